```python
import math
import jax, jax.numpy as jnp
from jax import lax
import numpy as np

D_MODEL = 4096
BATCH = 1
SEQ = 8192
DEPTH = 1

SSD_EXPAND = 2
SSD_D_INNER = SSD_EXPAND * D_MODEL
SSD_HEAD_DIM = 64
SSD_HEADS = SSD_D_INNER // SSD_HEAD_DIM
SSD_D_STATE = 128
SSD_GROUPS = 8
SSD_HEADS_PER_GROUP = SSD_HEADS // SSD_GROUPS
SSD_CONV = 5
SSD_CHUNK = 128
SSD_XBC = SSD_D_INNER + 2 * SSD_GROUPS * SSD_D_STATE
HG_HEAD_DIM = 128
HG_WIDTH = D_MODEL
HG_HEADS = HG_WIDTH // HG_HEAD_DIM
HG_CHUNK = 32
N_EXPERTS = 16
EC_CAPACITY_FACTOR = 2
D_FF_EXPERT = D_MODEL // 2
N_BRANCHES = 2
N_MOD = 6
RMS_EPS = 1e-6
IN_PROJ_SIZES = (SSD_D_INNER, SSD_XBC, SSD_HEADS, SSD_HEADS,
                 HG_WIDTH, HG_WIDTH, HG_WIDTH, HG_WIDTH, HG_WIDTH,
                 D_MODEL, D_MODEL)
D_IN_PROJ = sum(IN_PROJ_SIZES)

kernel_name = 'hybrid_ssd_hgrn2_ec_moe_block'


def rms_norm(x, w):
    xf = x.astype(jnp.float32)
    y = xf * lax.rsqrt(jnp.mean(xf * xf, axis=-1, keepdims=True) + RMS_EPS)
    return (y * w.astype(jnp.float32)).astype(x.dtype)


def split_in_proj(proj):
    parts, start = [], 0
    for size in IN_PROJ_SIZES:
        parts.append(proj[..., start:start + size])
        start += size
    return parts


def flip_seq(t):
    return jnp.flip(t, axis=1)


def segsum(a):
    T = a.shape[-1]
    rep = jnp.broadcast_to(a[..., None], a.shape + (T,))
    strict = jnp.tril(jnp.ones((T, T), dtype=bool), -1)
    cs = jnp.cumsum(jnp.where(strict, rep, 0.0), axis=-2)
    return jnp.where(jnp.tril(jnp.ones((T, T), dtype=bool)), cs, -jnp.inf)


def ssd_chunked(x, dt, a, b, c):
    Bsz, S = x.shape[:2]
    nc = S // SSD_CHUNK
    G, R, T = SSD_GROUPS, SSD_HEADS_PER_GROUP, SSD_CHUNK
    xdt = (x * dt[..., None]).reshape(Bsz, nc, T, G, R, SSD_HEAD_DIM)
    bc = b.reshape(Bsz, nc, T, G, SSD_D_STATE)
    cc = c.reshape(Bsz, nc, T, G, SSD_D_STATE)
    adt = jnp.moveaxis((dt * a).reshape(Bsz, nc, T, G, R), 2, -1)
    a_cs = jnp.cumsum(adt, axis=-1)
    decay = jnp.exp(segsum(adt))
    cb = jnp.einsum('bctgn,bcsgn->bcgts', cc, bc)
    y_diag = jnp.einsum('bcgts,bcgrts,bcsgrp->bctgrp', cb, decay, xdt)
    decay_states = jnp.exp(a_cs[..., -1:] - a_cs)
    states = jnp.einsum('bcsgn,bcgrs,bcsgrp->bcgrpn', bc, decay_states, xdt)
    chunk_log = jnp.moveaxis(a_cs[..., -1], 1, -1)
    chunk_log = jnp.pad(chunk_log, ((0, 0), (0, 0), (0, 0), (1, 0)))
    decay_chunk = jnp.exp(segsum(chunk_log))[..., :nc, :]
    states0 = jnp.concatenate([jnp.zeros_like(states[:, :1]), states], axis=1)
    prev_states = jnp.einsum('bgrzc,bcgrpn->bzgrpn', decay_chunk, states0)
    y_off = jnp.einsum('bctgn,bcgrpn,bcgrt->bctgrp', cc, prev_states, jnp.exp(a_cs))
    return (y_diag + y_off).reshape(Bsz, S, SSD_HEADS, SSD_HEAD_DIM)


def mamba2_branch(z, xbc, dt_f_raw, dt_b_raw, conv_w, conv_b, dt_bias_f, dt_bias_b,
                  a_log_f, a_log_b, d_skip, norm_w):
    dtype = z.dtype
    f32 = jnp.float32
    Bsz, S, _ = z.shape
    pad = SSD_CONV // 2
    xbc = lax.conv_general_dilated(xbc, conv_w[:, None, :].astype(xbc.dtype), window_strides=(1,),
                                   padding=[(pad, pad)], dimension_numbers=('NWC', 'WIO', 'NWC'),
                                   feature_group_count=SSD_XBC)
    xbc = jax.nn.silu((xbc + conv_b).astype(f32))
    nbc = SSD_GROUPS * SSD_D_STATE
    xs = xbc[..., :SSD_D_INNER].reshape(Bsz, S, SSD_HEADS, SSD_HEAD_DIM)
    bs = xbc[..., SSD_D_INNER:SSD_D_INNER + nbc].reshape(Bsz, S, SSD_GROUPS, SSD_D_STATE)
    cs = xbc[..., SSD_D_INNER + nbc:].reshape(Bsz, S, SSD_GROUPS, SSD_D_STATE)
    dt_f = jax.nn.softplus(dt_f_raw.astype(f32) + dt_bias_f.astype(f32))
    dt_b = jax.nn.softplus(dt_b_raw.astype(f32) + dt_bias_b.astype(f32))
    a_f = -jnp.exp(a_log_f.astype(f32))
    a_b = -jnp.exp(a_log_b.astype(f32))
    y_f = ssd_chunked(xs, dt_f, a_f, bs, cs)
    y_b = flip_seq(ssd_chunked(flip_seq(xs), flip_seq(dt_b), a_b, flip_seq(bs), flip_seq(cs)))
    y = y_f + y_b + d_skip.astype(f32)[:, None] * xs
    y = y.reshape(Bsz, S, SSD_D_INNER) * jax.nn.silu(z.astype(f32))
    yg = y.reshape(Bsz, S, SSD_GROUPS, SSD_D_INNER // SSD_GROUPS)
    yg = yg * lax.rsqrt(jnp.mean(yg * yg, axis=-1, keepdims=True) + RMS_EPS)
    return (yg.reshape(Bsz, S, SSD_D_INNER) * norm_w.astype(f32)).astype(dtype)


def hgrn2_chunked(q, k, v, log_f):
    Bsz, S = q.shape[:2]
    nc = S // HG_CHUNK
    T = HG_CHUNK

    def to_chunks(t):
        return jnp.moveaxis(t.reshape(Bsz, nc, T, HG_HEADS, HG_HEAD_DIM), 1, 0)

    causal_in_chunk = jnp.tril(jnp.ones((T, T), dtype=bool))[None, :, :, None, None]

    def step(state, inp):
        qc, kc, vc, gc = inp
        gcs = jnp.cumsum(gc, axis=1)
        o_inter = jnp.einsum('bthk,bhkv->bthv', qc * jnp.exp(gcs), state)
        rel = gcs[:, :, None] - gcs[:, None, :]
        rel = jnp.exp(jnp.where(causal_in_chunk, rel, -jnp.inf))
        scores = jnp.einsum('bthk,bshk,btshk->bhts', qc, kc, rel)
        o_intra = jnp.einsum('bhts,bshv->bthv', scores, vc)
        g_last = gcs[:, -1]
        k_dec = kc * jnp.exp(g_last[:, None] - gcs)
        state = state * jnp.exp(g_last)[..., None] + jnp.einsum('bshk,bshv->bhkv', k_dec, vc)
        return state, o_inter + o_intra

    state0 = jnp.zeros((Bsz, HG_HEADS, HG_HEAD_DIM, HG_HEAD_DIM), jnp.float32)
    _, o = lax.scan(step, state0, (to_chunks(q), to_chunks(k), to_chunks(v), to_chunks(log_f)))
    return jnp.moveaxis(o, 0, 1).reshape(Bsz, S, HG_WIDTH)


def hgrn2_branch(q, f_f_raw, f_b_raw, i, g, lb, norm_w):
    dtype = q.dtype
    f32 = jnp.float32
    Bsz, S, _ = q.shape
    shp = (Bsz, S, HG_HEADS, HG_HEAD_DIM)
    qf = q.astype(f32).reshape(shp)
    v = i.astype(f32).reshape(shp)

    def forget_terms(f_raw):
        fr = f_raw.astype(f32)
        log_f = jnp.log(lb + (1.0 - lb) * jax.nn.sigmoid(fr))
        k = (1.0 - lb) * jax.nn.sigmoid(-fr)
        return k.reshape(shp), log_f.reshape(shp)

    k_f, lf_f = forget_terms(f_f_raw)
    k_b, lf_b = forget_terms(f_b_raw)
    o_f = hgrn2_chunked(qf, k_f, v, lf_f)
    o_b = flip_seq(hgrn2_chunked(flip_seq(qf), flip_seq(k_b), flip_seq(v), flip_seq(lf_b)))
    o = o_f + o_b
    o = o * lax.rsqrt(jnp.mean(o * o, axis=-1, keepdims=True) + RMS_EPS) * norm_w.astype(f32)
    return (o * jax.nn.silu(g.astype(f32))).astype(dtype)


def token_mixer(h, w_in, conv_w, conv_b, dt_bias_f, dt_bias_b, a_log_f, a_log_b, d_skip,
                ssd_norm_w, lb, hg_norm_w, w_ssd_out, w_hg_out, w_mix_out):
    proj = jnp.einsum('bsd,de->bse', h, w_in)
    z, xbc, dtf, dtb, q, ff, fb, i, g, gate_a, gate_b = split_in_proj(proj)
    y_ssd = mamba2_branch(z, xbc, dtf, dtb, conv_w, conv_b, dt_bias_f, dt_bias_b,
                          a_log_f, a_log_b, d_skip, ssd_norm_w)
    y_hg = hgrn2_branch(q, ff, fb, i, g, lb, hg_norm_w)
    ya = jnp.einsum('bsi,id->bsd', y_ssd, w_ssd_out)
    yb = jnp.einsum('bsi,id->bsd', y_hg, w_hg_out)
    merged = jax.nn.sigmoid(gate_a) * ya + jax.nn.sigmoid(gate_b) * yb
    return jnp.einsum('bsd,de->bse', merged, w_mix_out)


def expert_choice_ffn(h, w_router, w_gate, w_up, w_down):
    Bsz, S, _ = h.shape
    cap = EC_CAPACITY_FACTOR * S // N_EXPERTS
    logits = jnp.einsum('bsd,de->bse', h, w_router).astype(jnp.float32)
    affinity = jax.nn.softmax(logits, axis=-1)
    gate_vals, tok_idx = lax.top_k(jnp.swapaxes(affinity, 1, 2), cap)
    bidx = jnp.arange(Bsz)[:, None, None]
    xg = h[bidx, tok_idx]
    hid = jax.nn.silu(jnp.einsum('becd,edf->becf', xg, w_gate)) * jnp.einsum('becd,edf->becf', xg, w_up)
    y = jnp.einsum('becf,efd->becd', hid, w_down) * gate_vals[..., None].astype(h.dtype)
    return jnp.zeros_like(h).at[bidx, tok_idx].add(y)


def setup_inputs(seed: int = 0) -> dict:
    key = jax.random.key(seed)
    ks = jax.random.split(key, 32)
    f32 = jnp.float32
    L = DEPTH

    def nrm(k, shape, scale):
        return jax.random.normal(k, shape, f32) * scale

    dt_f = jnp.exp(jax.random.uniform(ks[11], (L, SSD_HEADS), f32, math.log(1e-3), math.log(1e-1)))
    dt_b = jnp.exp(jax.random.uniform(ks[12], (L, SSD_HEADS), f32, math.log(1e-3), math.log(1e-1)))
    return {
        'x': nrm(ks[0], (BATCH, SEQ, D_MODEL), 1.0),
        'c': nrm(ks[1], (BATCH, D_MODEL), 1.0),
        'w_ada': nrm(ks[2], (L, D_MODEL, N_MOD * D_MODEL), 0.5 * D_MODEL ** -0.5),
        'b_ada': nrm(ks[3], (L, N_MOD * D_MODEL), 0.02),
        'norm_pre_mix': 1.0 + nrm(ks[4], (L, D_MODEL), 0.05),
        'norm_post_mix': 1.0 + nrm(ks[5], (L, D_MODEL), 0.05),
        'norm_pre_ffn': 1.0 + nrm(ks[6], (L, D_MODEL), 0.05),
        'norm_post_ffn': 1.0 + nrm(ks[7], (L, D_MODEL), 0.05),
        'w_in': nrm(ks[8], (L, D_MODEL, D_IN_PROJ), D_MODEL ** -0.5),
        'conv_w': nrm(ks[9], (L, SSD_CONV, SSD_XBC), SSD_CONV ** -0.5),
        'conv_b': nrm(ks[10], (L, SSD_XBC), 0.02),
        'dt_bias_fwd': dt_f + jnp.log(-jnp.expm1(-dt_f)),
        'dt_bias_bwd': dt_b + jnp.log(-jnp.expm1(-dt_b)),
        'a_log_fwd': jnp.log(jax.random.uniform(ks[13], (L, SSD_HEADS), f32, 1.0, 16.0)),
        'a_log_bwd': jnp.log(jax.random.uniform(ks[14], (L, SSD_HEADS), f32, 1.0, 16.0)),
        'd_skip': 1.0 + nrm(ks[15], (L, SSD_HEADS), 0.1),
        'ssd_norm_w': 1.0 + nrm(ks[16], (L, SSD_D_INNER), 0.05),
        'hg_lower_bound': nrm(ks[17], (DEPTH + 1, HG_WIDTH), 1.0),
        'hg_norm_w': 1.0 + nrm(ks[18], (L, HG_WIDTH), 0.05),
        'w_ssd_out': nrm(ks[19], (L, SSD_D_INNER, D_MODEL), SSD_D_INNER ** -0.5),
        'w_hg_out': nrm(ks[20], (L, HG_WIDTH, D_MODEL), HG_WIDTH ** -0.5),
        'w_mix_out': nrm(ks[21], (L, D_MODEL, D_MODEL), D_MODEL ** -0.5),
        'w_router': nrm(ks[22], (L, D_MODEL, N_EXPERTS), D_MODEL ** -0.5),
        'w_gate': nrm(ks[23], (L, N_EXPERTS, D_MODEL, D_FF_EXPERT), D_MODEL ** -0.5),
        'w_up': nrm(ks[24], (L, N_EXPERTS, D_MODEL, D_FF_EXPERT), D_MODEL ** -0.5),
        'w_down': nrm(ks[25], (L, N_EXPERTS, D_FF_EXPERT, D_MODEL), D_FF_EXPERT ** -0.5),
    }


def reference(x, c, w_ada, b_ada, norm_pre_mix, norm_post_mix, norm_pre_ffn, norm_post_ffn,
              w_in, conv_w, conv_b, dt_bias_fwd, dt_bias_bwd, a_log_fwd, a_log_bwd, d_skip,
              ssd_norm_w, hg_lower_bound, hg_norm_w, w_ssd_out, w_hg_out, w_mix_out,
              w_router, w_gate, w_up, w_down):
    lower_bounds = jnp.cumsum(jax.nn.softmax(hg_lower_bound.astype(jnp.float32), axis=0), axis=0)
    c_act = jax.nn.silu(c)
    for l in range(DEPTH):
        mod = jnp.einsum('bd,de->be', c_act, w_ada[l]) + b_ada[l]
        sh_m, sc_m, g_m, sh_f, sc_f, g_f = jnp.split(mod[:, None, :], N_MOD, axis=-1)
        h = rms_norm(x, norm_pre_mix[l]) * (1.0 + sc_m) + sh_m
        y = token_mixer(h, w_in[l], conv_w[l], conv_b[l], dt_bias_fwd[l], dt_bias_bwd[l],
                        a_log_fwd[l], a_log_bwd[l], d_skip[l], ssd_norm_w[l], lower_bounds[l],
                        hg_norm_w[l], w_ssd_out[l], w_hg_out[l], w_mix_out[l])
        x = x + g_m * rms_norm(y, norm_post_mix[l])
        h = rms_norm(x, norm_pre_ffn[l]) * (1.0 + sc_f) + sh_f
        y = expert_choice_ffn(h, w_router[l], w_gate[l], w_up[l], w_down[l])
        x = x + g_f * rms_norm(y, norm_post_ffn[l])
    return x
```

```python
import functools

import numpy as np
import jax
import jax.numpy as jnp
from jax import lax
from jax.experimental import pallas as pl
from jax.experimental.pallas import tpu as pltpu

f32 = jnp.float32
bf16 = jnp.bfloat16

D_MODEL = 4096
SSD_D_INNER = 8192
SSD_HEADS = 128
SSD_HEAD_DIM = 64
SSD_D_STATE = 128
SSD_GROUPS = 8
SSD_HEADS_PER_GROUP = 16
SSD_CONV = 5
SSD_XBC = SSD_D_INNER + 2 * SSD_GROUPS * SSD_D_STATE
HG_HEADS = 32
HG_HEAD_DIM = 128
N_EXPERTS = 16
D_FF_EXPERT = 2048
RMS_EPS = 1e-6

OFF_Z = 0
OFF_XBC = OFF_Z + SSD_D_INNER
OFF_DT = OFF_XBC + SSD_XBC
OFF_Q = OFF_DT + 2 * SSD_HEADS
OFF_FF = OFF_Q + D_MODEL
OFF_FB = OFF_FF + D_MODEL
OFF_I = OFF_FB + D_MODEL
OFF_G = OFF_I + D_MODEL
OFF_GA = OFF_G + D_MODEL
OFF_GB = OFF_GA + D_MODEL
D_IN_PROJ = OFF_GB + D_MODEL

LANE = 128
CHUNK = 128
VMEM_LIMIT = 56 * 1024 * 1024


def _params(sem, vmem=None):
    return pltpu.CompilerParams(dimension_semantics=sem, vmem_limit_bytes=vmem)


def _sigmoid(x):
    return 1.0 / (1.0 + jnp.exp(-x))


def _silu(x):
    return x * _sigmoid(x)


def _softplus(x):
    return jnp.maximum(x, 0.0) + jnp.log(1.0 + jnp.exp(-jnp.abs(x)))


def _split2(x):
    hi = x.astype(bf16)
    lo = (x - hi.astype(f32)).astype(bf16)
    return hi, lo


def _split3(x):
    hi = x.astype(bf16)
    r = x - hi.astype(f32)
    mid = r.astype(bf16)
    lo = (r - mid.astype(f32)).astype(bf16)
    return hi, mid, lo


def _dot(a, b):
    return jnp.dot(a, b, preferred_element_type=f32)


def _dot_nt(a, b):
    return lax.dot_general(a, b, (((1,), (1,)), ((), ())), preferred_element_type=f32)


def _ada_kernel(c_ref, w_ref, b_ref, o_ref):
    c = c_ref[...]
    ca = jnp.broadcast_to(_silu(c), (8, c.shape[1]))
    chi, clo = _split2(ca)
    whi, wlo = _split2(w_ref[...])
    acc = _dot(chi, whi) + _dot(chi, wlo) + _dot(clo, whi)
    o_ref[...] = acc[0:1, :] + b_ref[...]


def _ada(c, w, b):
    d, n = w.shape
    tn = 512
    return pl.pallas_call(
        _ada_kernel,
        grid=(n // tn,),
        in_specs=[pl.BlockSpec((1, d), lambda j: (0, 0)),
                  pl.BlockSpec((d, tn), lambda j: (0, j)),
                  pl.BlockSpec((1, tn), lambda j: (0, j))],
        out_specs=pl.BlockSpec((1, tn), lambda j: (0, j)),
        out_shape=jax.ShapeDtypeStruct((1, n), f32),
        compiler_params=_params(("parallel",), VMEM_LIMIT),
        name="adaln",
    )(c, w, b.reshape(1, n))


def _prenorm_kernel(x_ref, w_ref, sc_ref, sh_ref, o_ref):
    x = x_ref[...]
    r = lax.rsqrt(jnp.mean(x * x, axis=-1, keepdims=True) + RMS_EPS)
    h = (x * r * w_ref[...]) * (1.0 + sc_ref[...]) + sh_ref[...]
    o_ref[...] = h.astype(o_ref.dtype)


def _prenorm(x, w, sc, sh):
    s, d = x.shape
    tm = 256
    row = pl.BlockSpec((1, d), lambda i: (0, 0))
    return pl.pallas_call(
        _prenorm_kernel,
        grid=(s // tm,),
        in_specs=[pl.BlockSpec((tm, d), lambda i: (i, 0)), row, row, row],
        out_specs=pl.BlockSpec((tm, d), lambda i: (i, 0)),
        out_shape=jax.ShapeDtypeStruct((s, d), bf16),
        compiler_params=_params(("parallel",)),
        name="prenorm",
    )(x, w.reshape(1, d), sc, sh)


def _mm_kernel(*refs, n_extra, nk, epilogue):
    a_ref, w_ref = refs[0], refs[1]
    extras = refs[2:2 + n_extra]
    o_ref = refs[2 + n_extra]
    acc_ref = refs[3 + n_extra]
    k = pl.program_id(2)
    part = _dot(a_ref[...], w_ref[...].astype(bf16))

    @pl.when(k == 0)
    def _():
        acc_ref[...] = part

    @pl.when(k > 0)
    def _():
        acc_ref[...] += part

    @pl.when(k == nk - 1)
    def _():
        o_ref[...] = epilogue(acc_ref[...], *extras).astype(o_ref.dtype)


def _matmul(a, w, *, tm, tn, tk, out_dtype, extras=(), epilogue=None, name="matmul"):
    m, kdim = a.shape
    n = w.shape[1]
    nk = kdim // tk
    if epilogue is None:
        epilogue = lambda acc: acc
    extra_arrays = [e[0] for e in extras]
    extra_specs = [e[1] for e in extras]
    return pl.pallas_call(
        functools.partial(_mm_kernel, n_extra=len(extras), nk=nk, epilogue=epilogue),
        grid=(m // tm, n // tn, nk),
        in_specs=[pl.BlockSpec((tm, tk), lambda i, j, k: (i, k)),
                  pl.BlockSpec((tk, tn), lambda i, j, k: (k, j))] + extra_specs,
        out_specs=pl.BlockSpec((tm, tn), lambda i, j, k: (i, j)),
        out_shape=jax.ShapeDtypeStruct((m, n), out_dtype),
        scratch_shapes=[pltpu.VMEM((tm, tn), f32)],
        compiler_params=_params(("parallel", "parallel", "arbitrary"), VMEM_LIMIT),
        name=name,
    )(a, w, *extra_arrays)


CONV_HALO = 16


def _conv_kernel(x_ref, w_ref, b_ref, o_ref, *, seq, tile):
    w = w_ref[...]
    b = b_ref[...]
    nt = seq // tile
    ext_rows = tile + 2 * CONV_HALO

    def body(i, carry):
        r0 = pl.multiple_of(i * tile, tile)
        cur = x_ref[pl.ds(r0, tile), :].astype(f32)
        p0 = pl.multiple_of(jnp.maximum(r0 - CONV_HALO, 0), CONV_HALO)
        n0 = pl.multiple_of(jnp.minimum(r0 + tile, seq - CONV_HALO), CONV_HALO)
        prev = jnp.where(i > 0, x_ref[pl.ds(p0, CONV_HALO), :].astype(f32), 0.0)
        nxt = jnp.where(i < nt - 1, x_ref[pl.ds(n0, CONV_HALO), :].astype(f32), 0.0)
        ext = jnp.concatenate([prev, cur, nxt], axis=0)
        acc = b + w[2:3, :] * cur
        for k in (0, 1, 3, 4):
            d = k - SSD_CONV // 2
            shifted = pltpu.roll(ext, (-d) % ext_rows, axis=0)[CONV_HALO:CONV_HALO + tile, :]
            acc = acc + w[k:k + 1, :] * shifted
        o_ref[pl.ds(r0, tile), :] = _silu(acc).astype(o_ref.dtype)
        return carry

    lax.fori_loop(0, nt, body, 0)


def _conv(proj, conv_w, conv_b):
    s = proj.shape[0]
    cw = 256
    off = OFF_XBC // cw
    tile = min(512, s)
    return pl.pallas_call(
        functools.partial(_conv_kernel, seq=s, tile=tile),
        grid=(SSD_XBC // cw,),
        in_specs=[pl.BlockSpec((s, cw), lambda j: (0, off + j)),
                  pl.BlockSpec((SSD_CONV, cw), lambda j: (0, j)),
                  pl.BlockSpec((1, cw), lambda j: (0, j))],
        out_specs=pl.BlockSpec((s, cw), lambda j: (0, j)),
        out_shape=jax.ShapeDtypeStruct((s, SSD_XBC), bf16),
        compiler_params=_params(("parallel",), VMEM_LIMIT),
        name="conv_silu",
    )(proj, conv_w, conv_b.reshape(1, SSD_XBC))


def _ssd_consts():
    L = CHUNK
    t = np.arange(L)
    tri_f = (t[None, :] <= t[:, None]).astype(np.float32)
    tri_b = (t[None, :] >= t[:, None]).astype(np.float32)
    h = np.arange(SSD_HEADS)
    g, r = h // SSD_HEADS_PER_GROUP, h % SSD_HEADS_PER_GROUP
    sel_a = np.zeros((3 * SSD_HEADS, SSD_GROUPS * LANE), np.float32)
    for j in range(3):
        sel_a[j * SSD_HEADS + h, g * LANE + j * SSD_HEADS_PER_GROUP + r] = 1.0
    sel_w = np.zeros((6 * SSD_HEADS, SSD_GROUPS * LANE), np.float32)
    for j in range(6):
        sel_w[j * SSD_HEADS + h, g * LANE + j * SSD_HEADS_PER_GROUP + r] = 1.0
    ea = np.zeros((LANE, SSD_HEADS_PER_GROUP * L), np.float32)
    for j in range(3):
        for rr in range(SSD_HEADS_PER_GROUP):
            ea[j * SSD_HEADS_PER_GROUP + rr, rr * L:(rr + 1) * L] = 1.0
    gw = SSD_HEADS_PER_GROUP * SSD_HEAD_DIM
    ew = np.zeros((LANE, 3 * gw), np.float32)
    for j in range(6):
        for rr in range(SSD_HEADS_PER_GROUP):
            ew[j * SSD_HEADS_PER_GROUP + rr,
               (j // 2) * gw + rr * SSD_HEAD_DIM:(j // 2) * gw + (rr + 1) * SSD_HEAD_DIM] = 1.0
    return tri_f, tri_b, sel_a, sel_w, ea, ew


def _ssd_prep_kernel(dt_ref, bias_ref, alog_ref, trif_ref, trib_ref, sela_ref, selw_ref,
                     paf_ref, pwf_ref, arf_ref, pab_ref, pwb_ref, arb_ref):
    raw = dt_ref[...].astype(f32)
    L = raw.shape[0]
    outs = ((paf_ref, pwf_ref, arf_ref, trif_ref, L - 1), (pab_ref, pwb_ref, arb_ref, trib_ref, 0))
    for d, (pa_ref, pw_ref, ar_ref, tri_ref, tot_row) in enumerate(outs):
        sl = slice(d * SSD_HEADS, (d + 1) * SSD_HEADS)
        dt = _softplus(raw[:, sl] + bias_ref[:, sl])
        adt = dt * (-jnp.exp(alog_ref[:, sl]))
        h3 = jnp.concatenate(_split3(adt), axis=1)
        a3 = _dot(tri_ref[...], h3)
        a = (a3[:, :SSD_HEADS] + a3[:, SSD_HEADS:2 * SSD_HEADS]) + a3[:, 2 * SSD_HEADS:]
        atot = a[tot_row:tot_row + 1, :]
        wgt = jnp.exp(atot - a) * dt
        ea = jnp.exp(a)
        ar_ref[...] = a.T
        pa_ref[...] = _dot(jnp.concatenate(_split3(a), axis=1), sela_ref[...]).astype(bf16)
        cols = _split2(wgt) + _split2(ea) + _split2(dt)
        pw_ref[...] = _dot(jnp.concatenate(cols, axis=1), selw_ref[...]).astype(bf16)


def _ssd_prep(proj, dt_bias, a_log, consts):
    s = proj.shape[0]
    L = CHUNK
    tri_f, tri_b, sel_a, sel_w = consts
    gl = SSD_GROUPS * LANE
    const = lambda shape: pl.BlockSpec(shape, lambda c: (0, 0))
    packed = pl.BlockSpec((L, gl), lambda c: (c, 0))
    rowf = pl.BlockSpec((SSD_HEADS, L), lambda c: (0, c))
    return pl.pallas_call(
        _ssd_prep_kernel,
        grid=(s // L,),
        in_specs=[pl.BlockSpec((L, 2 * SSD_HEADS), lambda c: (c, OFF_DT // (2 * SSD_HEADS))),
                  const((1, 2 * SSD_HEADS)), const((1, 2 * SSD_HEADS)),
                  const((L, L)), const((L, L)),
                  const(sel_a.shape), const(sel_w.shape)],
        out_specs=[packed, packed, rowf, packed, packed, rowf],
        out_shape=[jax.ShapeDtypeStruct((s, gl), bf16), jax.ShapeDtypeStruct((s, gl), bf16),
                   jax.ShapeDtypeStruct((SSD_HEADS, s), f32)] * 2,
        compiler_params=_params(("parallel",)),
        name="ssd_prep",
    )(proj, dt_bias, a_log, tri_f, tri_b, sel_a, sel_w)


def _ssd_kernel(*refs, rev, final):
    if final:
        (x_ref, b_ref, c_ref, pa_ref, pw_ref, ar_ref, ea_ref, ew_ref,
         yprev_ref, z_ref, nw_ref, dsk_ref, o_ref, st_ref) = refs
    else:
        x_ref, b_ref, c_ref, pa_ref, pw_ref, ar_ref, ea_ref, ew_ref, o_ref, st_ref = refs
    L = CHUNK
    gw = SSD_HEADS_PER_GROUP * SSD_HEAD_DIM
    npair = SSD_HEADS_PER_GROUP // 2

    @pl.when(pl.program_id(1) == 0)
    def _():
        st_ref[...] = jnp.zeros_like(st_ref)

    x = x_ref[...]
    bm = b_ref[...]
    cm = c_ref[...]
    aexp = _dot(pa_ref[...], ea_ref[...])
    wed = _dot(pw_ref[...], ew_ref[...])
    wexp, eexp, dtexp = wed[:, :gw], wed[:, gw:2 * gw], wed[:, 2 * gw:]
    xf = x.astype(f32)
    xdt = (xf * dtexp).astype(bf16)
    xw = (xf * wexp).astype(bf16)
    tot_row = 0 if rev else L - 1
    dec_tot = eexp[tot_row:tot_row + 1, :]
    bt = bm.astype(f32).T.astype(bf16)
    cb = _dot(cm, bt)
    ti = lax.broadcasted_iota(jnp.int32, (L, L), 0)
    si = lax.broadcasted_iota(jnp.int32, (L, L), 1)
    keep = (si >= ti) if rev else (si <= ti)
    cbm = jnp.where(keep, cb, 0.0)
    lane = lax.broadcasted_iota(jnp.int32, (L, LANE), 1)
    first = lane < SSD_HEAD_DIM
    ys = []
    for j in range(npair):
        ds_ = []
        for r in (2 * j, 2 * j + 1):
            diff = aexp[:, r * L:(r + 1) * L] - ar_ref[r:r + 1, :]
            ds_.append((jnp.exp(jnp.minimum(diff, 0.0)) * cbm).astype(bf16))
        mp = jnp.concatenate(ds_, axis=1)
        xp = xdt[:, j * LANE:(j + 1) * LANE]
        zero = jnp.zeros_like(xp)
        xbd = jnp.concatenate([jnp.where(first, xp, zero), jnp.where(first, zero, xp)], axis=0)
        st = st_ref[j]
        y = _dot(mp, xbd) + _dot(cm, st.astype(bf16)) * eexp[:, j * LANE:(j + 1) * LANE]
        ys.append(y)
        st_ref[j] = st * dec_tot[:, j * LANE:(j + 1) * LANE] + _dot(bt, xw[:, j * LANE:(j + 1) * LANE])
    y = jnp.concatenate(ys, axis=1)
    if final:
        y = y + yprev_ref[...].astype(f32) + dsk_ref[...] * xf
        y = y * _silu(z_ref[...].astype(f32))
        y = y * lax.rsqrt(jnp.mean(y * y, axis=-1, keepdims=True) + RMS_EPS) * nw_ref[...]
    o_ref[...] = y.astype(o_ref.dtype)


def _ssd(xact, pa, pw, arow, ea, ew, *, rev, final_args=None):
    s = xact.shape[0]
    L = CHUNK
    nc = s // L
    gw = SSD_HEADS_PER_GROUP * SSD_HEAD_DIM
    cc = (lambda c: nc - 1 - c) if rev else (lambda c: c)
    nxg = SSD_D_INNER // LANE
    in_specs = [pl.BlockSpec((L, gw), lambda g, c: (cc(c), g)),
                pl.BlockSpec((L, LANE), lambda g, c: (cc(c), nxg + g)),
                pl.BlockSpec((L, LANE), lambda g, c: (cc(c), nxg + SSD_GROUPS + g)),
                pl.BlockSpec((L, LANE), lambda g, c: (cc(c), g)),
                pl.BlockSpec((L, LANE), lambda g, c: (cc(c), g)),
                pl.BlockSpec((SSD_HEADS_PER_GROUP, L), lambda g, c: (g, cc(c))),
                pl.BlockSpec(ea.shape, lambda g, c: (0, 0)),
                pl.BlockSpec(ew.shape, lambda g, c: (0, 0))]
    args = [xact, xact, xact, pa, pw, arow, ea, ew]
    final = final_args is not None
    if final:
        yprev, proj, norm_w, dskip = final_args
        in_specs += [pl.BlockSpec((L, gw), lambda g, c: (cc(c), g)),
                     pl.BlockSpec((L, gw), lambda g, c: (cc(c), OFF_Z // gw + g)),
                     pl.BlockSpec((1, gw), lambda g, c: (0, g)),
                     pl.BlockSpec((1, gw), lambda g, c: (0, g))]
        args += [yprev, proj, norm_w, dskip]
    return pl.pallas_call(
        functools.partial(_ssd_kernel, rev=rev, final=final),
        grid=(SSD_GROUPS, nc),
        in_specs=in_specs,
        out_specs=pl.BlockSpec((L, gw), lambda g, c: (cc(c), g)),
        out_shape=jax.ShapeDtypeStruct((s, SSD_D_INNER), bf16),
        scratch_shapes=[pltpu.VMEM((SSD_HEADS_PER_GROUP // 2, SSD_D_STATE, LANE), f32)],
        compiler_params=_params(("parallel", "arbitrary")),
        name="ssd_bwd" if rev else "ssd_fwd",
    )(*args)


HG_LEVELS = 8


def _hg_consts(rev):
    C = CHUNK
    t = np.arange(C)
    if rev:
        tri = (t[None, :] >= t[:, None]).astype(np.float32)
    else:
        tri = (t[None, :] <= t[:, None]).astype(np.float32)
    tmats = [tri]
    masks = [(t[:, None] == t[None, :]).astype(np.float32)]
    for lv in range(1, HG_LEVELS):
        b = 1 << (lv - 1)
        blk = t // (2 * b)
        upper = (t % (2 * b)) >= b
        if rev:
            m = blk * 2 * b + b
            mask = (blk[:, None] == blk[None, :]) & (~upper)[:, None] & upper[None, :]
        else:
            m = blk * 2 * b + b - 1
            mask = (blk[:, None] == blk[None, :]) & upper[:, None] & (~upper)[None, :]
        tmats.append(tri - tri[m])
        masks.append(mask.astype(np.float32))
    return np.concatenate(tmats, axis=0), np.stack(masks, axis=0)


def _hg_kernel(*refs, rev, has_prev):
    if has_prev:
        q_ref, f_ref, v_ref, lb_ref, t_ref, m_ref, prev_ref, o_ref, st_ref = refs
    else:
        q_ref, f_ref, v_ref, lb_ref, t_ref, m_ref, o_ref, st_ref = refs
    C = CHUNK

    @pl.when(pl.program_id(1) == 0)
    def _():
        st_ref[...] = jnp.zeros_like(st_ref)

    qf = q_ref[...].astype(f32)
    fr = f_ref[...].astype(f32)
    vb = v_ref[...]
    lb = lb_ref[...]
    lf = jnp.log(lb + (1.0 - lb) * _sigmoid(fr))
    kk = (1.0 - lb) * _sigmoid(-fr)
    lf3 = jnp.concatenate(_split3(lf), axis=1)
    al3 = _dot(t_ref[...], lf3)
    al = (al3[:, :LANE] + al3[:, LANE:2 * LANE]) + al3[:, 2 * LANE:]
    g = al[:C, :]
    sc = _dot_nt(qf.astype(bf16), kk.astype(bf16)) * m_ref[0]
    for lv in range(1, HG_LEVELS):
        xd = jnp.exp(-jnp.abs(al[lv * C:(lv + 1) * C, :]))
        sc = sc + _dot_nt((qf * xd).astype(bf16), (kk * xd).astype(bf16)) * m_ref[lv]
    o = _dot(sc.astype(bf16), vb)
    st = st_ref[...]
    o = o + _dot_nt((qf * jnp.exp(g)).astype(bf16), st.astype(bf16))
    tot_row = 0 if rev else C - 1
    gtot = g[tot_row:tot_row + 1, :]
    kdec = (kk * jnp.exp(gtot - g)).astype(bf16)
    vt = vb.astype(f32).T.astype(bf16)
    st_ref[...] = st * jnp.exp(gtot) + _dot(vt, kdec)
    if has_prev:
        o = o + prev_ref[...].astype(f32)
    o_ref[...] = o.astype(o_ref.dtype)


def _hgrn(proj, lb, tcat, masks, *, rev, prev=None):
    s = proj.shape[0]
    C = CHUNK
    nc = s // C
    cc = (lambda c: nc - 1 - c) if rev else (lambda c: c)
    f_off = (OFF_FB if rev else OFF_FF) // LANE
    blk = lambda off: pl.BlockSpec((C, LANE), lambda h, c: (cc(c), off + h))
    in_specs = [blk(OFF_Q // LANE), blk(f_off), blk(OFF_I // LANE),
                pl.BlockSpec((1, LANE), lambda h, c: (0, h)),
                pl.BlockSpec(tcat.shape, lambda h, c: (0, 0)),
                pl.BlockSpec(masks.shape, lambda h, c: (0, 0, 0))]
    args = [proj, proj, proj, lb, tcat, masks]
    if prev is not None:
        in_specs.append(blk(0))
        args.append(prev)
    return pl.pallas_call(
        functools.partial(_hg_kernel, rev=rev, has_prev=prev is not None),
        grid=(HG_HEADS, nc),
        in_specs=in_specs,
        out_specs=blk(0),
        out_shape=jax.ShapeDtypeStruct((s, D_MODEL), bf16),
        scratch_shapes=[pltpu.VMEM((HG_HEAD_DIM, HG_HEAD_DIM), f32)],
        compiler_params=_params(("parallel", "arbitrary")),
        name="hgrn_bwd" if rev else "hgrn_fwd",
    )(*args)


HG_GATE_COLS = 256


def _hg_out_kernel(o_ref, g_ref, w_ref, y_ref, rs_ref):
    j = pl.program_id(1)

    @pl.when(j == 0)
    def _():
        o = o_ref[...].astype(f32)
        rs_ref[...] = lax.rsqrt(jnp.mean(o * o, axis=-1, keepdims=True) + RMS_EPS)

    c0 = pl.multiple_of(j * HG_GATE_COLS, HG_GATE_COLS)
    o = o_ref[:, pl.ds(c0, HG_GATE_COLS)].astype(f32)
    y = o * rs_ref[...] * w_ref[...] * _silu(g_ref[...].astype(f32))
    y_ref[...] = y.astype(y_ref.dtype)


def _hg_out(o, proj, norm_w):
    s, d = o.shape
    tm = 256
    cw = HG_GATE_COLS
    return pl.pallas_call(
        _hg_out_kernel,
        grid=(s // tm, d // cw),
        in_specs=[pl.BlockSpec((tm, d), lambda i, j: (i, 0)),
                  pl.BlockSpec((tm, cw), lambda i, j: (i, OFF_G // cw + j)),
                  pl.BlockSpec((1, cw), lambda i, j: (0, j))],
        out_specs=pl.BlockSpec((tm, cw), lambda i, j: (i, j)),
        out_shape=jax.ShapeDtypeStruct((s, d), bf16),
        scratch_shapes=[pltpu.VMEM((tm, 1), f32)],
        compiler_params=_params(("parallel", "arbitrary")),
        name="hgrn_out",
    )(o, proj, norm_w.reshape(1, d))


def _post_mix_kernel(y_ref, x_ref, gm_ref, wpost_ref, wpre_ref, sc_ref, sh_ref, wr_ref,
                     x1_ref, hp_ref, lg_ref):
    y = y_ref[...]
    yn = y * lax.rsqrt(jnp.mean(y * y, axis=-1, keepdims=True) + RMS_EPS) * wpost_ref[...]
    x1 = x_ref[...] + gm_ref[...] * yn
    x1_ref[...] = x1
    h = x1 * lax.rsqrt(jnp.mean(x1 * x1, axis=-1, keepdims=True) + RMS_EPS) * wpre_ref[...]
    h = h * (1.0 + sc_ref[...]) + sh_ref[...]
    hhi, hlo = _split2(h)
    whi, wlo = _split2(wr_ref[...])
    lg_ref[...] = _dot(hhi, whi) + _dot(hhi, wlo) + _dot(hlo, whi)
    half = h.shape[1] // 2
    top = pltpu.bitcast(hhi[:, :half].astype(f32), jnp.uint32)
    bot = pltpu.bitcast(hhi[:, half:].astype(f32), jnp.uint32)
    hp_ref[...] = top | (bot >> 16)


def _post_mix(y, x, gm, wpost, wpre, sc, sh, wr_pad):
    s, d = x.shape
    tm = 256
    row = pl.BlockSpec((1, d), lambda i: (0, 0))
    tile = pl.BlockSpec((tm, d), lambda i: (i, 0))
    return pl.pallas_call(
        _post_mix_kernel,
        grid=(s // tm,),
        in_specs=[tile, tile, row, row, row, row, row, pl.BlockSpec((d, LANE), lambda i: (0, 0))],
        out_specs=[tile, pl.BlockSpec((tm, d // 2), lambda i: (i, 0)),
                   pl.BlockSpec((tm, LANE), lambda i: (i, 0))],
        out_shape=[jax.ShapeDtypeStruct((s, d), f32), jax.ShapeDtypeStruct((s, d // 2), jnp.uint32),
                   jax.ShapeDtypeStruct((s, LANE), f32)],
        compiler_params=_params(("parallel",), VMEM_LIMIT),
        name="post_mix",
    )(y, x, gm, wpost.reshape(1, d), wpre.reshape(1, d), sc, sh, wr_pad)


def _select_kernel(lg_ref, utri_ref, idx_ref, gate_ref, pos_ref, sel_ref,
                   aff_ref, cin_ref, *, seq, cap):
    E = N_EXPERTS
    nb = seq // LANE
    lg = lg_ref[...]
    lane = lax.broadcasted_iota(jnp.int32, lg.shape, 1)
    lg = jnp.where(lane < E, lg, -jnp.inf)
    mx = jnp.max(lg, axis=-1, keepdims=True)
    ex = jnp.exp(lg - mx)
    aff = ex / jnp.sum(ex, axis=-1, keepdims=True)
    aff_t = aff.T[:E, :]
    aff_ref[...] = aff_t
    bits = pltpu.bitcast(aff_t, jnp.int32)

    def search(i, thr):
        cand = thr | (jnp.int32(1) << (30 - i))
        cnt = jnp.sum((bits >= cand).astype(jnp.int32), axis=-1, keepdims=True)
        return jnp.where(cnt >= cap, cand, thr)

    thr = lax.fori_loop(0, 31, search, jnp.zeros((E, 1), jnp.int32))
    gt = bits > thr
    eq = bits == thr
    need = cap - jnp.sum(gt.astype(jnp.int32), axis=-1, keepdims=True)
    utri = utri_ref[...]

    def cumsum_excl(mask_f):
        carry = jnp.zeros((E, 1), f32)
        outs = []
        for b in range(nb):
            blk = mask_f[:, b * LANE:(b + 1) * LANE]
            inc = _dot(blk.astype(bf16), utri)
            outs.append(inc - blk + carry)
            carry = carry + inc[:, LANE - 1:LANE]
        return jnp.concatenate(outs, axis=1)

    eq_f = eq.astype(f32)
    rank_eq = cumsum_excl(eq_f)
    sel = gt | (eq & (rank_eq < need.astype(f32)))
    sel_f = sel.astype(f32)
    pos = cumsum_excl(sel_f)
    pos_ref[...] = pos
    sel_ref[...] = sel_f
    cin_ref[...] = jnp.where(sel, pos + 1.0, 0.0)
    pcol = lax.broadcasted_iota(jnp.int32, (cap, LANE), 0).astype(f32)
    lanes = lax.broadcasted_iota(jnp.int32, (cap, LANE), 1)
    idx_m = jnp.zeros((cap, LANE), f32)
    gate_m = jnp.zeros((cap, LANE), f32)
    for e in range(E):
        def blk_body(b, carry):
            cnt_acc, g_acc = carry
            c0 = pl.multiple_of(b * LANE, LANE)
            posr = pos_ref[pl.ds(e, 1), pl.ds(c0, LANE)]
            selr = sel_ref[pl.ds(e, 1), pl.ds(c0, LANE)]
            cinr = cin_ref[pl.ds(e, 1), pl.ds(c0, LANE)]
            affr = aff_ref[pl.ds(e, 1), pl.ds(c0, LANE)]
            incl = posr + selr
            cnt_acc = cnt_acc + jnp.where(incl <= pcol, 1.0, 0.0)
            g_acc = g_acc + jnp.where(cinr == pcol + 1.0, affr, 0.0)
            return cnt_acc, g_acc

        zero = jnp.zeros((cap, LANE), f32)
        cnt_acc, g_acc = lax.fori_loop(0, nb, blk_body, (zero, zero))
        idx_col = jnp.sum(cnt_acc, axis=-1, keepdims=True)
        gate_col = jnp.sum(g_acc, axis=-1, keepdims=True)
        idx_m = jnp.where(lanes == e, idx_col, idx_m)
        gate_m = jnp.where(lanes == e, gate_col, gate_m)
    idx_ref[...] = idx_m.astype(jnp.int32)
    gate_ref[...] = gate_m


def _select(logits, cap):
    s = logits.shape[0]
    t = np.arange(LANE)
    utri = jnp.asarray((t[:, None] <= t[None, :]).astype(np.float32), bf16)
    full = lambda shape: pl.BlockSpec(shape, lambda: tuple(0 for _ in shape))
    return pl.pallas_call(
        functools.partial(_select_kernel, seq=s, cap=cap),
        in_specs=[full((s, LANE)), full((LANE, LANE))],
        out_specs=[full((cap, LANE)), full((cap, LANE)), full((N_EXPERTS, s)), full((N_EXPERTS, s))],
        out_shape=[jax.ShapeDtypeStruct((cap, LANE), jnp.int32), jax.ShapeDtypeStruct((cap, LANE), f32),
                   jax.ShapeDtypeStruct((N_EXPERTS, s), f32), jax.ShapeDtypeStruct((N_EXPERTS, s), f32)],
        scratch_shapes=[pltpu.VMEM((N_EXPERTS, s), f32), pltpu.VMEM((N_EXPERTS, s), f32)],
        compiler_params=_params(None, VMEM_LIMIT),
        name="ec_select",
    )(logits, utri)


FF_TILE = 256
DOWN_TILE = 512


def _expert_kernel(idx_ref, hp_ref, wg_ref, wu_ref, wd_ref, gate_ref, y_ref,
                   xbuf_ref, xa_ref, xb_ref, hid_ref, sem, *, cap):
    e = pl.program_id(0)
    s = pl.program_id(1)
    n_ff = D_FF_EXPERT // FF_TILE
    half = D_MODEL // 2

    def row_copy(p):
        return pltpu.make_async_copy(hp_ref.at[idx_ref[e, p]], xbuf_ref.at[p], sem)

    @pl.when(s == 0)
    def _():
        def start(p, c):
            row_copy(p).start()
            return c

        def wait(p, c):
            row_copy(p).wait()
            return c

        lax.fori_loop(0, cap, start, 0)
        lax.fori_loop(0, cap, wait, 0)
        for j in range(half // LANE):
            u = xbuf_ref[:, j, :]
            xa_ref[:, j * LANE:(j + 1) * LANE] = pltpu.bitcast(u & jnp.uint32(0xFFFF0000), f32).astype(bf16)
            xb_ref[:, j * LANE:(j + 1) * LANE] = pltpu.bitcast(u << 16, f32).astype(bf16)

    @pl.when(s < n_ff)
    def _():
        wg = wg_ref[0].astype(bf16)
        wu = wu_ref[0].astype(bf16)
        xa = xa_ref[...]
        xb = xb_ref[...]
        hg = _dot(xa, wg[:half]) + _dot(xb, wg[half:])
        hu = _dot(xa, wu[:half]) + _dot(xb, wu[half:])
        c0 = pl.multiple_of(s * FF_TILE, FF_TILE)
        hid_ref[:, pl.ds(c0, FF_TILE)] = (_silu(hg) * hu).astype(bf16)

    @pl.when(s >= n_ff)
    def _():
        y = _dot(hid_ref[...], wd_ref[0].astype(bf16))
        y_ref[0] = (y * gate_ref[0]).astype(y_ref.dtype)


def _experts(idx, hp, w_gate, w_up, w_down, gate, cap):
    n_ff = D_FF_EXPERT // FF_TILE
    n_dn = D_MODEL // DOWN_TILE
    ffj = lambda s: jnp.minimum(s, n_ff - 1)
    dnj = lambda s: jnp.maximum(s - n_ff, 0)
    grid_spec = pltpu.PrefetchScalarGridSpec(
        num_scalar_prefetch=1,
        grid=(N_EXPERTS, n_ff + n_dn),
        in_specs=[pl.BlockSpec(memory_space=pl.ANY),
                  pl.BlockSpec((1, D_MODEL, FF_TILE), lambda e, s, idx: (e, 0, ffj(s))),
                  pl.BlockSpec((1, D_MODEL, FF_TILE), lambda e, s, idx: (e, 0, ffj(s))),
                  pl.BlockSpec((1, D_FF_EXPERT, DOWN_TILE), lambda e, s, idx: (e, 0, dnj(s))),
                  pl.BlockSpec((1, cap, 1), lambda e, s, idx: (e, 0, 0))],
        out_specs=pl.BlockSpec((1, cap, DOWN_TILE), lambda e, s, idx: (e, 0, dnj(s))),
        scratch_shapes=[pltpu.VMEM((cap, D_MODEL // 2 // LANE, LANE), jnp.uint32),
                        pltpu.VMEM((cap, D_MODEL // 2), bf16),
                        pltpu.VMEM((cap, D_MODEL // 2), bf16),
                        pltpu.VMEM((cap, D_FF_EXPERT), bf16),
                        pltpu.SemaphoreType.DMA(())],
    )
    return pl.pallas_call(
        functools.partial(_expert_kernel, cap=cap),
        grid_spec=grid_spec,
        out_shape=jax.ShapeDtypeStruct((N_EXPERTS, cap, D_MODEL), bf16),
        compiler_params=_params(("arbitrary", "arbitrary"), VMEM_LIMIT),
        name="ec_experts",
    )(idx, hp, w_gate, w_up, w_down, gate)


COMBINE_ALIGN = 16


def _combine_kernel(off_ref, y_ref, pos_ref, sel_ref, x1_ref, gf_ref, w_ref, o_ref,
                    buf_ref, acc_ref, sems, *, tt):
    i = pl.program_id(0)
    rows = tt + COMBINE_ALIGN

    def chunk_copy(e, slot):
        start = pl.multiple_of(off_ref[e, i], COMBINE_ALIGN)
        return pltpu.make_async_copy(y_ref.at[e, pl.ds(start, rows), :],
                                     buf_ref.at[slot], sems.at[slot])

    chunk_copy(0, 0).start()
    jrow = lax.broadcasted_iota(jnp.int32, (rows, tt), 0).astype(f32)
    for e in range(N_EXPERTS):
        slot = e % 2
        if e + 1 < N_EXPERTS:
            chunk_copy(e + 1, 1 - slot).start()
        rel = pos_ref[e:e + 1, :] - off_ref[e, i].astype(f32)
        et = jnp.where((sel_ref[e:e + 1, :] > 0.0) & (rel == jrow), 1.0, 0.0)
        chunk_copy(e, slot).wait()
        part = lax.dot_general(et.astype(bf16), buf_ref[slot], (((0,), (0,)), ((), ())),
                               preferred_element_type=f32)
        if e == 0:
            acc_ref[...] = part
        else:
            acc_ref[...] += part
    y = acc_ref[...]
    yn = y * lax.rsqrt(jnp.mean(y * y, axis=-1, keepdims=True) + RMS_EPS) * w_ref[...]
    o_ref[...] = x1_ref[...] + gf_ref[...] * yn


def _combine(off, yexp, pos, sel, x1, gf, w_post):
    s, d = x1.shape
    tt = pos.shape[1] // off.shape[1]
    row = pl.BlockSpec((1, d), lambda i, off: (0, 0))
    grid_spec = pltpu.PrefetchScalarGridSpec(
        num_scalar_prefetch=1,
        grid=(s // tt,),
        in_specs=[pl.BlockSpec(memory_space=pl.ANY),
                  pl.BlockSpec((N_EXPERTS, tt), lambda i, off: (0, i)),
                  pl.BlockSpec((N_EXPERTS, tt), lambda i, off: (0, i)),
                  pl.BlockSpec((tt, d), lambda i, off: (i, 0)), row, row],
        out_specs=pl.BlockSpec((tt, d), lambda i, off: (i, 0)),
        scratch_shapes=[pltpu.VMEM((2, tt + COMBINE_ALIGN, d), bf16), pltpu.VMEM((tt, d), f32),
                        pltpu.SemaphoreType.DMA((2,))],
    )
    return pl.pallas_call(
        functools.partial(_combine_kernel, tt=tt),
        grid_spec=grid_spec,
        out_shape=jax.ShapeDtypeStruct((s, d), f32),
        compiler_params=_params(("arbitrary",), VMEM_LIMIT),
        name="ec_combine",
    )(off, yexp, pos, sel, x1, gf, w_post.reshape(1, d))


def _gate_specs(off, tm, tn):
    q = 256
    n_sub = tn // q
    return [pl.BlockSpec((tm, q), functools.partial(
        lambda i, j, k, u: (i, off // q + j * n_sub + u), u=u)) for u in range(n_sub)]


def _merge_a(acc, *gate_refs):
    ga = jnp.concatenate([r[...] for r in gate_refs], axis=1).astype(f32)
    return _sigmoid(ga) * acc


def _merge_b(acc, part_ref, *gate_refs):
    gb = jnp.concatenate([r[...] for r in gate_refs], axis=1).astype(f32)
    return part_ref[...] + _sigmoid(gb) * acc


def _layer(x, mod, lb, norm_pre_mix, norm_post_mix, norm_pre_ffn, norm_post_ffn,
           w_in, conv_w, conv_b, dt_bias, a_log, d_skip, ssd_norm_w, hg_norm_w,
           w_ssd_out, w_hg_out, w_mix_out, w_router, w_gate, w_up, w_down):
    s, d = x.shape
    sh_m, sc_m, g_m, sh_f, sc_f, g_f = [mod[:, i * d:(i + 1) * d] for i in range(6)]

    h = _prenorm(x, norm_pre_mix, sc_m, sh_m)
    tm_in = min(2048, s)
    proj = _matmul(h, w_in, tm=tm_in, tn=1280, tk=1024, out_dtype=bf16, name="in_proj")

    xact = _conv(proj, conv_w, conv_b)
    tri_f, tri_b, sel_a, sel_w, ea, ew = _ssd_consts()
    cb = lambda a: jnp.asarray(a, bf16)
    paf, pwf, arf, pab, pwb, arb = _ssd_prep(proj, dt_bias, a_log,
                                             (cb(tri_f), cb(tri_b), cb(sel_a), cb(sel_w)))
    y_f = _ssd(xact, paf, pwf, arf, cb(ea), cb(ew), rev=False)
    dskip = jnp.repeat(d_skip, SSD_HEAD_DIM).reshape(1, SSD_D_INNER)
    y_ssd = _ssd(xact, pab, pwb, arb, cb(ea), cb(ew), rev=True,
                 final_args=(y_f, proj, ssd_norm_w.reshape(1, SSD_D_INNER), dskip))

    tf, mf = _hg_consts(False)
    tb, mb = _hg_consts(True)
    o_f = _hgrn(proj, lb, cb(tf), jnp.asarray(mf), rev=False)
    o = _hgrn(proj, lb, cb(tb), jnp.asarray(mb), rev=True, prev=o_f)
    y_hg = _hg_out(o, proj, hg_norm_w)

    tm = min(1024, s)
    tn = 1024
    part = _matmul(y_ssd, w_ssd_out, tm=tm, tn=tn, tk=1024, out_dtype=f32,
                   extras=[(proj, sp) for sp in _gate_specs(OFF_GA, tm, tn)],
                   epilogue=_merge_a, name="ssd_out_proj")
    merged = _matmul(y_hg, w_hg_out, tm=tm, tn=tn, tk=1024, out_dtype=bf16,
                     extras=[(part, pl.BlockSpec((tm, tn), lambda i, j, k: (i, j)))]
                     + [(proj, sp) for sp in _gate_specs(OFF_GB, tm, tn)],
                     epilogue=_merge_b, name="hg_out_proj")
    y_mix = _matmul(merged, w_mix_out, tm=tm, tn=tn, tk=1024, out_dtype=f32, name="mix_out_proj")

    wr_pad = jnp.pad(w_router, ((0, 0), (0, LANE - N_EXPERTS)))
    x1, hp, logits = _post_mix(y_mix, x, g_m, norm_post_mix, norm_pre_ffn, sc_f, sh_f, wr_pad)
    cap = 2 * s // N_EXPERTS
    idx_m, gate_m, pos, sel = _select(logits, cap)
    idx = idx_m[:, :N_EXPERTS].T
    gate = gate_m[:, :N_EXPERTS].T.reshape(N_EXPERTS, cap, 1)
    hp3 = hp.reshape(s, d // 2 // LANE, LANE)
    yexp = _experts(idx, hp3, w_gate, w_up, w_down, gate, cap)
    tt = min(128, cap // 2)
    first_pos = pos[:, ::tt].astype(jnp.int32)
    off = jnp.minimum(first_pos // COMBINE_ALIGN * COMBINE_ALIGN, cap - (tt + COMBINE_ALIGN))
    return _combine(off, yexp, pos, sel, x1, g_f, norm_post_ffn)


def kernel(x, c, w_ada, b_ada, norm_pre_mix, norm_post_mix, norm_pre_ffn, norm_post_ffn, w_in, conv_w, conv_b, dt_bias_fwd, dt_bias_bwd, a_log_fwd, a_log_bwd, d_skip, ssd_norm_w, hg_lower_bound, hg_norm_w, w_ssd_out, w_hg_out, w_mix_out, w_router, w_gate, w_up, w_down):
    depth = w_ada.shape[0]
    lower_bounds = jnp.cumsum(jax.nn.softmax(hg_lower_bound.astype(f32), axis=0), axis=0)
    outs = []
    for bi in range(x.shape[0]):
        xb = x[bi]
        cb_ = c[bi:bi + 1]
        for l in range(depth):
            mod = _ada(cb_, w_ada[l], b_ada[l])
            dt_bias = jnp.concatenate([dt_bias_fwd[l], dt_bias_bwd[l]]).reshape(1, -1)
            a_log = jnp.concatenate([a_log_fwd[l], a_log_bwd[l]]).reshape(1, -1)
            xb = _layer(xb, mod, lower_bounds[l].reshape(1, -1), norm_pre_mix[l], norm_post_mix[l],
                        norm_pre_ffn[l], norm_post_ffn[l], w_in[l], conv_w[l], conv_b[l],
                        dt_bias, a_log, d_skip[l], ssd_norm_w[l], hg_norm_w[l],
                        w_ssd_out[l], w_hg_out[l], w_mix_out[l], w_router[l],
                        w_gate[l], w_up[l], w_down[l])
        outs.append(xb)
    return jnp.stack(outs, axis=0)
```

```python
import functools

import numpy as np
import jax
import jax.numpy as jnp
from jax import lax
from jax.experimental import pallas as pl
from jax.experimental.pallas import tpu as pltpu

f32 = jnp.float32
bf16 = jnp.bfloat16

D_MODEL = 4096
SSD_D_INNER = 8192
SSD_HEADS = 128
SSD_HEAD_DIM = 64
SSD_D_STATE = 128
SSD_GROUPS = 8
SSD_HEADS_PER_GROUP = 16
SSD_CONV = 5
SSD_XBC = SSD_D_INNER + 2 * SSD_GROUPS * SSD_D_STATE
HG_HEADS = 32
HG_HEAD_DIM = 128
N_EXPERTS = 16
D_FF_EXPERT = 2048
RMS_EPS = 1e-6

OFF_Z = 0
OFF_XBC = OFF_Z + SSD_D_INNER
OFF_DT = OFF_XBC + SSD_XBC
OFF_Q = OFF_DT + 2 * SSD_HEADS
OFF_FF = OFF_Q + D_MODEL
OFF_FB = OFF_FF + D_MODEL
OFF_I = OFF_FB + D_MODEL
OFF_G = OFF_I + D_MODEL
OFF_GA = OFF_G + D_MODEL
OFF_GB = OFF_GA + D_MODEL
D_IN_PROJ = OFF_GB + D_MODEL

LANE = 128
CHUNK = 128
VMEM_LIMIT = 56 * 1024 * 1024


def _params(sem, vmem=None):
    return pltpu.CompilerParams(dimension_semantics=sem, vmem_limit_bytes=vmem)


def _sigmoid(x):
    return 1.0 / (1.0 + jnp.exp(-x))


def _silu(x):
    return x * _sigmoid(x)


def _softplus(x):
    return jnp.maximum(x, 0.0) + jnp.log(1.0 + jnp.exp(-jnp.abs(x)))


def _split2(x):
    hi = x.astype(bf16)
    lo = (x - hi.astype(f32)).astype(bf16)
    return hi, lo


def _split3(x):
    hi = x.astype(bf16)
    r = x - hi.astype(f32)
    mid = r.astype(bf16)
    lo = (r - mid.astype(f32)).astype(bf16)
    return hi, mid, lo


def _dot(a, b):
    return jnp.dot(a, b, preferred_element_type=f32)


def _dot_nt(a, b):
    return lax.dot_general(a, b, (((1,), (1,)), ((), ())), preferred_element_type=f32)


def _ada_kernel(c_ref, w_ref, b_ref, o_ref):
    c = c_ref[...]
    ca = jnp.broadcast_to(_silu(c), (8, c.shape[1]))
    chi, clo = _split2(ca)
    whi, wlo = _split2(w_ref[...])
    acc = _dot(chi, whi) + _dot(chi, wlo) + _dot(clo, whi)
    o_ref[...] = acc[0:1, :] + b_ref[...]


def _ada(c, w, b):
    d, n = w.shape
    tn = 512
    return pl.pallas_call(
        _ada_kernel,
        grid=(n // tn,),
        in_specs=[pl.BlockSpec((1, d), lambda j: (0, 0)),
                  pl.BlockSpec((d, tn), lambda j: (0, j)),
                  pl.BlockSpec((1, tn), lambda j: (0, j))],
        out_specs=pl.BlockSpec((1, tn), lambda j: (0, j)),
        out_shape=jax.ShapeDtypeStruct((1, n), f32),
        compiler_params=_params(("parallel",), VMEM_LIMIT),
        name="adaln",
    )(c, w, b.reshape(1, n))


def _prenorm_kernel(x_ref, w_ref, sc_ref, sh_ref, o_ref):
    x = x_ref[...]
    r = lax.rsqrt(jnp.mean(x * x, axis=-1, keepdims=True) + RMS_EPS)
    h = (x * r * w_ref[...]) * (1.0 + sc_ref[...]) + sh_ref[...]
    o_ref[...] = h.astype(o_ref.dtype)


def _prenorm(x, w, sc, sh):
    s, d = x.shape
    tm = 256
    row = pl.BlockSpec((1, d), lambda i: (0, 0))
    return pl.pallas_call(
        _prenorm_kernel,
        grid=(s // tm,),
        in_specs=[pl.BlockSpec((tm, d), lambda i: (i, 0)), row, row, row],
        out_specs=pl.BlockSpec((tm, d), lambda i: (i, 0)),
        out_shape=jax.ShapeDtypeStruct((s, d), bf16),
        compiler_params=_params(("parallel",)),
        name="prenorm",
    )(x, w.reshape(1, d), sc, sh)


def _wsmm_kernel(*refs, n_extra, nn, epilogue):
    a_ref, w_ref = refs[0], refs[1]
    extras = refs[2:2 + n_extra]
    o_ref = refs[2 + n_extra]
    wbuf_ref = refs[3 + n_extra]
    n = pl.program_id(0)
    m = pl.program_id(1)
    kc = w_ref.shape[0]

    @pl.when(n < nn)
    def _():
        r0 = pl.multiple_of(m * kc, kc)
        wbuf_ref[n % 2, pl.ds(r0, kc), :] = w_ref[...].astype(bf16)

    @pl.when(n == 0)
    def _():
        o_ref[...] = jnp.zeros_like(o_ref)

    @pl.when(n > 0)
    def _():
        acc = _dot(a_ref[...], wbuf_ref[(n - 1) % 2])
        o_ref[...] = epilogue(acc, *extras).astype(o_ref.dtype)


def _wsmm(a, w, *, tm, tn, out_dtype, extras=(), epilogue=None, name="matmul"):
    m, kdim = a.shape
    n = w.shape[1]
    nm, nn = m // tm, n // tn
    kc = kdim // nm
    if epilogue is None:
        epilogue = lambda acc: acc
    jn = lambda nidx: jnp.maximum(nidx - 1, 0)
    extra_arrays = [e[0] for e in extras]
    extra_specs = [pl.BlockSpec(e[1], functools.partial(lambda nidx, midx, f: f(midx, jn(nidx)), f=e[2]))
                   for e in extras]
    return pl.pallas_call(
        functools.partial(_wsmm_kernel, n_extra=len(extras), nn=nn, epilogue=epilogue),
        grid=(nn + 1, nm),
        in_specs=[pl.BlockSpec((tm, kdim), lambda nidx, midx: (midx, 0)),
                  pl.BlockSpec((kc, tn), lambda nidx, midx: (midx, jnp.minimum(nidx, nn - 1)))] + extra_specs,
        out_specs=pl.BlockSpec((tm, tn), lambda nidx, midx: (jnp.where(nidx == 0, 0, midx), jn(nidx))),
        out_shape=jax.ShapeDtypeStruct((m, n), out_dtype),
        scratch_shapes=[pltpu.VMEM((2, kdim, tn), bf16)],
        compiler_params=_params(("arbitrary", "arbitrary"), VMEM_LIMIT),
        name=name,
    )(a, w, *extra_arrays)


CONV_HALO = 16


def _conv_kernel(x_ref, w_ref, b_ref, o_ref, *, seq, tile):
    w = w_ref[...]
    b = b_ref[...]
    nt = seq // tile
    ext_rows = tile + 2 * CONV_HALO

    def body(i, carry):
        r0 = pl.multiple_of(i * tile, tile)
        cur = x_ref[pl.ds(r0, tile), :].astype(f32)
        p0 = pl.multiple_of(jnp.maximum(r0 - CONV_HALO, 0), CONV_HALO)
        n0 = pl.multiple_of(jnp.minimum(r0 + tile, seq - CONV_HALO), CONV_HALO)
        prev = jnp.where(i > 0, x_ref[pl.ds(p0, CONV_HALO), :].astype(f32), 0.0)
        nxt = jnp.where(i < nt - 1, x_ref[pl.ds(n0, CONV_HALO), :].astype(f32), 0.0)
        ext = jnp.concatenate([prev, cur, nxt], axis=0)
        acc = b + w[2:3, :] * cur
        for k in (0, 1, 3, 4):
            d = k - SSD_CONV // 2
            shifted = pltpu.roll(ext, (-d) % ext_rows, axis=0)[CONV_HALO:CONV_HALO + tile, :]
            acc = acc + w[k:k + 1, :] * shifted
        o_ref[pl.ds(r0, tile), :] = _silu(acc).astype(o_ref.dtype)
        return carry

    lax.fori_loop(0, nt, body, 0)


def _conv(proj, conv_w, conv_b):
    s = proj.shape[0]
    cw = 256
    off = OFF_XBC // cw
    tile = min(512, s)
    return pl.pallas_call(
        functools.partial(_conv_kernel, seq=s, tile=tile),
        grid=(SSD_XBC // cw,),
        in_specs=[pl.BlockSpec((s, cw), lambda j: (0, off + j)),
                  pl.BlockSpec((SSD_CONV, cw), lambda j: (0, j)),
                  pl.BlockSpec((1, cw), lambda j: (0, j))],
        out_specs=pl.BlockSpec((s, cw), lambda j: (0, j)),
        out_shape=jax.ShapeDtypeStruct((s, SSD_XBC), bf16),
        compiler_params=_params(("parallel",), VMEM_LIMIT),
        name="conv_silu",
    )(proj, conv_w, conv_b.reshape(1, SSD_XBC))


def _ssd_consts():
    L = CHUNK
    t = np.arange(L)
    tri_f = (t[None, :] <= t[:, None]).astype(np.float32)
    tri_b = (t[None, :] >= t[:, None]).astype(np.float32)
    h = np.arange(SSD_HEADS)
    g, r = h // SSD_HEADS_PER_GROUP, h % SSD_HEADS_PER_GROUP
    sel_a = np.zeros((3 * SSD_HEADS, SSD_GROUPS * LANE), np.float32)
    for j in range(3):
        sel_a[j * SSD_HEADS + h, g * LANE + j * SSD_HEADS_PER_GROUP + r] = 1.0
    sel_w = np.zeros((6 * SSD_HEADS, SSD_GROUPS * LANE), np.float32)
    for j in range(6):
        sel_w[j * SSD_HEADS + h, g * LANE + j * SSD_HEADS_PER_GROUP + r] = 1.0
    ea = np.zeros((LANE, SSD_HEADS_PER_GROUP * L), np.float32)
    for j in range(3):
        for rr in range(SSD_HEADS_PER_GROUP):
            ea[j * SSD_HEADS_PER_GROUP + rr, rr * L:(rr + 1) * L] = 1.0
    gw = SSD_HEADS_PER_GROUP * SSD_HEAD_DIM
    ew = np.zeros((LANE, 3 * gw), np.float32)
    for j in range(6):
        for rr in range(SSD_HEADS_PER_GROUP):
            ew[j * SSD_HEADS_PER_GROUP + rr,
               (j // 2) * gw + rr * SSD_HEAD_DIM:(j // 2) * gw + (rr + 1) * SSD_HEAD_DIM] = 1.0
    return tri_f, tri_b, sel_a, sel_w, ea, ew


def _ssd_prep_kernel(dt_ref, bias_ref, alog_ref, trif_ref, trib_ref, sela_ref, selw_ref,
                     paf_ref, pwf_ref, arf_ref, pab_ref, pwb_ref, arb_ref):
    raw = dt_ref[...].astype(f32)
    L = raw.shape[0]
    outs = ((paf_ref, pwf_ref, arf_ref, trif_ref, L - 1), (pab_ref, pwb_ref, arb_ref, trib_ref, 0))
    for d, (pa_ref, pw_ref, ar_ref, tri_ref, tot_row) in enumerate(outs):
        sl = slice(d * SSD_HEADS, (d + 1) * SSD_HEADS)
        dt = _softplus(raw[:, sl] + bias_ref[:, sl])
        adt = dt * (-jnp.exp(alog_ref[:, sl]))
        h3 = jnp.concatenate(_split3(adt), axis=1)
        a3 = _dot(tri_ref[...], h3)
        a = (a3[:, :SSD_HEADS] + a3[:, SSD_HEADS:2 * SSD_HEADS]) + a3[:, 2 * SSD_HEADS:]
        atot = a[tot_row:tot_row + 1, :]
        wgt = jnp.exp(atot - a) * dt
        ea = jnp.exp(a)
        ar_ref[...] = a.T
        pa_ref[...] = _dot(jnp.concatenate(_split3(a), axis=1), sela_ref[...]).astype(bf16)
        cols = _split2(wgt) + _split2(ea) + _split2(dt)
        pw_ref[...] = _dot(jnp.concatenate(cols, axis=1), selw_ref[...]).astype(bf16)


def _ssd_prep(proj, dt_bias, a_log, consts):
    s = proj.shape[0]
    L = CHUNK
    tri_f, tri_b, sel_a, sel_w = consts
    gl = SSD_GROUPS * LANE
    const = lambda shape: pl.BlockSpec(shape, lambda c: (0, 0))
    packed = pl.BlockSpec((L, gl), lambda c: (c, 0))
    rowf = pl.BlockSpec((SSD_HEADS, L), lambda c: (0, c))
    return pl.pallas_call(
        _ssd_prep_kernel,
        grid=(s // L,),
        in_specs=[pl.BlockSpec((L, 2 * SSD_HEADS), lambda c: (c, OFF_DT // (2 * SSD_HEADS))),
                  const((1, 2 * SSD_HEADS)), const((1, 2 * SSD_HEADS)),
                  const((L, L)), const((L, L)),
                  const(sel_a.shape), const(sel_w.shape)],
        out_specs=[packed, packed, rowf, packed, packed, rowf],
        out_shape=[jax.ShapeDtypeStruct((s, gl), bf16), jax.ShapeDtypeStruct((s, gl), bf16),
                   jax.ShapeDtypeStruct((SSD_HEADS, s), f32)] * 2,
        compiler_params=_params(("parallel",)),
        name="ssd_prep",
    )(proj, dt_bias, a_log, tri_f, tri_b, sel_a, sel_w)


def _ssd_kernel(*refs, rev, final):
    if final:
        (x_ref, b_ref, c_ref, pa_ref, pw_ref, ar_ref, ea_ref, ew_ref,
         yprev_ref, z_ref, nw_ref, dsk_ref, o_ref, st_ref) = refs
    else:
        x_ref, b_ref, c_ref, pa_ref, pw_ref, ar_ref, ea_ref, ew_ref, o_ref, st_ref = refs
    L = CHUNK
    gw = SSD_HEADS_PER_GROUP * SSD_HEAD_DIM
    npair = SSD_HEADS_PER_GROUP // 2

    @pl.when(pl.program_id(1) == 0)
    def _():
        st_ref[...] = jnp.zeros_like(st_ref)

    x = x_ref[...]
    bm = b_ref[...]
    cm = c_ref[...]
    aexp = _dot(pa_ref[...], ea_ref[...])
    wed = _dot(pw_ref[...], ew_ref[...])
    wexp, eexp, dtexp = wed[:, :gw], wed[:, gw:2 * gw], wed[:, 2 * gw:]
    xf = x.astype(f32)
    xdt = (xf * dtexp).astype(bf16)
    xw = (xf * wexp).astype(bf16)
    tot_row = 0 if rev else L - 1
    dec_tot = eexp[tot_row:tot_row + 1, :]
    bt = bm.astype(f32).T.astype(bf16)
    cb = _dot(cm, bt)
    ti = lax.broadcasted_iota(jnp.int32, (L, L), 0)
    si = lax.broadcasted_iota(jnp.int32, (L, L), 1)
    keep = (si >= ti) if rev else (si <= ti)
    cbm = jnp.where(keep, cb, 0.0)
    lane = lax.broadcasted_iota(jnp.int32, (L, LANE), 1)
    first = lane < SSD_HEAD_DIM
    ys = []
    for j in range(npair):
        ds_ = []
        for r in (2 * j, 2 * j + 1):
            diff = aexp[:, r * L:(r + 1) * L] - ar_ref[r:r + 1, :]
            ds_.append((jnp.exp(jnp.minimum(diff, 0.0)) * cbm).astype(bf16))
        mp = jnp.concatenate(ds_, axis=1)
        xp = xdt[:, j * LANE:(j + 1) * LANE]
        zero = jnp.zeros_like(xp)
        xbd = jnp.concatenate([jnp.where(first, xp, zero), jnp.where(first, zero, xp)], axis=0)
        st = st_ref[j]
        y = _dot(mp, xbd) + _dot(cm, st.astype(bf16)) * eexp[:, j * LANE:(j + 1) * LANE]
        ys.append(y)
        st_ref[j] = st * dec_tot[:, j * LANE:(j + 1) * LANE] + _dot(bt, xw[:, j * LANE:(j + 1) * LANE])
    y = jnp.concatenate(ys, axis=1)
    if final:
        y = y + yprev_ref[...].astype(f32) + dsk_ref[...] * xf
        y = y * _silu(z_ref[...].astype(f32))
        y = y * lax.rsqrt(jnp.mean(y * y, axis=-1, keepdims=True) + RMS_EPS) * nw_ref[...]
    o_ref[...] = y.astype(o_ref.dtype)


def _ssd(xact, pa, pw, arow, ea, ew, *, rev, final_args=None):
    s = xact.shape[0]
    L = CHUNK
    nc = s // L
    gw = SSD_HEADS_PER_GROUP * SSD_HEAD_DIM
    cc = (lambda c: nc - 1 - c) if rev else (lambda c: c)
    nxg = SSD_D_INNER // LANE
    in_specs = [pl.BlockSpec((L, gw), lambda g, c: (cc(c), g)),
                pl.BlockSpec((L, LANE), lambda g, c: (cc(c), nxg + g)),
                pl.BlockSpec((L, LANE), lambda g, c: (cc(c), nxg + SSD_GROUPS + g)),
                pl.BlockSpec((L, LANE), lambda g, c: (cc(c), g)),
                pl.BlockSpec((L, LANE), lambda g, c: (cc(c), g)),
                pl.BlockSpec((SSD_HEADS_PER_GROUP, L), lambda g, c: (g, cc(c))),
                pl.BlockSpec(ea.shape, lambda g, c: (0, 0)),
                pl.BlockSpec(ew.shape, lambda g, c: (0, 0))]
    args = [xact, xact, xact, pa, pw, arow, ea, ew]
    final = final_args is not None
    if final:
        yprev, proj, norm_w, dskip = final_args
        in_specs += [pl.BlockSpec((L, gw), lambda g, c: (cc(c), g)),
                     pl.BlockSpec((L, gw), lambda g, c: (cc(c), OFF_Z // gw + g)),
                     pl.BlockSpec((1, gw), lambda g, c: (0, g)),
                     pl.BlockSpec((1, gw), lambda g, c: (0, g))]
        args += [yprev, proj, norm_w, dskip]
    return pl.pallas_call(
        functools.partial(_ssd_kernel, rev=rev, final=final),
        grid=(SSD_GROUPS, nc),
        in_specs=in_specs,
        out_specs=pl.BlockSpec((L, gw), lambda g, c: (cc(c), g)),
        out_shape=jax.ShapeDtypeStruct((s, SSD_D_INNER), bf16),
        scratch_shapes=[pltpu.VMEM((SSD_HEADS_PER_GROUP // 2, SSD_D_STATE, LANE), f32)],
        compiler_params=_params(("parallel", "arbitrary")),
        name="ssd_bwd" if rev else "ssd_fwd",
    )(*args)


HG_LEVELS = 8
HG_MM_LEVELS = 2
HG_PAIR = 2


def _hg_level_ref_row(lv, blk, rev):
    b = 1 << (lv - 1)
    return blk * 2 * b + (b if rev else b - 1)


def _hg_consts(rev):
    C = CHUNK
    t = np.arange(C)
    if rev:
        tri = (t[None, :] >= t[:, None]).astype(np.float32)
    else:
        tri = (t[None, :] <= t[:, None]).astype(np.float32)
    tmats = []
    masks = [(t[:, None] == t[None, :]).astype(np.float32)]
    for lv in range(1, HG_LEVELS):
        b = 1 << (lv - 1)
        blk = t // (2 * b)
        upper = (t % (2 * b)) >= b
        if rev:
            mask = (blk[:, None] == blk[None, :]) & (~upper)[:, None] & upper[None, :]
        else:
            mask = (blk[:, None] == blk[None, :]) & upper[:, None] & (~upper)[None, :]
        if lv <= HG_MM_LEVELS:
            tmats.append(tri - tri[_hg_level_ref_row(lv, blk, rev)])
        masks.append(mask.astype(np.float32))
    return tri, np.concatenate(tmats, axis=0), np.stack(masks, axis=0)


def _hg_chain(qb, fb, vb, lb, tri_ref, tsm_ref, m_ref, g_ref, st_ref, rev):
    C = CHUNK
    qf = qb.astype(f32)
    fr = fb.astype(f32)
    lf = jnp.log(lb + (1.0 - lb) * _sigmoid(fr))
    kk = (1.0 - lb) * _sigmoid(-fr)
    h3 = _split3(lf)
    g3 = _dot(tri_ref[...], jnp.concatenate(h3, axis=1))
    g = (g3[:, :LANE] + g3[:, LANE:2 * LANE]) + g3[:, 2 * LANE:]
    e2 = _dot(tsm_ref[...], jnp.concatenate(h3[:2], axis=1))
    e_small = e2[:, :LANE] + e2[:, LANE:]
    g_ref[...] = g
    sc = _dot_nt(qb, kk.astype(bf16)) * m_ref[0]
    for lv in range(1, HG_LEVELS):
        if lv <= HG_MM_LEVELS:
            e = e_small[(lv - 1) * C:lv * C, :]
        else:
            b = 1 << (lv - 1)
            rows = []
            for blk in range(C // (2 * b)):
                m = _hg_level_ref_row(lv, blk, rev)
                rows.append(jnp.broadcast_to(g_ref[m:m + 1, :], (2 * b, LANE)))
            e = g - jnp.concatenate(rows, axis=0)
        xd = jnp.exp(-jnp.abs(e))
        sc = sc + _dot_nt((qf * xd).astype(bf16), (kk * xd).astype(bf16)) * m_ref[lv]
    o = _dot(sc.astype(bf16), vb)
    st = st_ref[...]
    o = o + _dot_nt((qf * jnp.exp(g)).astype(bf16), st.astype(bf16))
    tot_row = 0 if rev else C - 1
    gtot = g[tot_row:tot_row + 1, :]
    kdec = (kk * jnp.exp(gtot - g)).astype(bf16)
    vt = vb.astype(f32).T.astype(bf16)
    st_ref[...] = st * jnp.exp(gtot) + _dot(vt, kdec)
    return o


def _hg_kernel(qf_ref, ff_ref, vf_ref, qb_ref, fb_ref, vb_ref, lb_ref,
               trif_ref, tsmf_ref, mf_ref, trib_ref, tsmb_ref, mb_ref,
               of_ref, ob_ref, st_ref, g_ref):
    @pl.when(pl.program_id(1) == 0)
    def _():
        st_ref[...] = jnp.zeros_like(st_ref)

    dirs = ((qf_ref, ff_ref, vf_ref, trif_ref, tsmf_ref, mf_ref, of_ref, False),
            (qb_ref, fb_ref, vb_ref, trib_ref, tsmb_ref, mb_ref, ob_ref, True))
    for d, (q_ref, f_ref, v_ref, tri_ref, tsm_ref, m_ref, o_ref, rev) in enumerate(dirs):
        for hh in range(HG_PAIR):
            sl = slice(hh * LANE, (hh + 1) * LANE)
            k = d * HG_PAIR + hh
            o = _hg_chain(q_ref[:, sl], f_ref[:, sl], v_ref[:, sl], lb_ref[:, sl],
                          tri_ref, tsm_ref, m_ref, g_ref.at[k], st_ref.at[k], rev)
            o_ref[:, sl] = o.astype(o_ref.dtype)


def _hgrn(proj, lb, consts_f, consts_b):
    s = proj.shape[0]
    C = CHUNK
    nc = s // C
    w = HG_PAIR * LANE
    fwd = lambda off: pl.BlockSpec((C, w), lambda h, c: (c, off // w + h))
    bwd = lambda off: pl.BlockSpec((C, w), lambda h, c: (nc - 1 - c, off // w + h))
    const = lambda a: pl.BlockSpec(a.shape, lambda h, c: tuple(0 for _ in a.shape))
    consts = list(consts_f) + list(consts_b)
    return pl.pallas_call(
        _hg_kernel,
        grid=(HG_HEADS // HG_PAIR, nc),
        in_specs=[fwd(OFF_Q), fwd(OFF_FF), fwd(OFF_I), bwd(OFF_Q), bwd(OFF_FB), bwd(OFF_I),
                  pl.BlockSpec((1, w), lambda h, c: (0, h))] + [const(a) for a in consts],
        out_specs=[fwd(0), bwd(0)],
        out_shape=[jax.ShapeDtypeStruct((s, D_MODEL), bf16)] * 2,
        scratch_shapes=[pltpu.VMEM((2 * HG_PAIR, HG_HEAD_DIM, HG_HEAD_DIM), f32),
                        pltpu.VMEM((2 * HG_PAIR, C, LANE), f32)],
        compiler_params=_params(("parallel", "arbitrary")),
        name="hgrn",
    )(proj, proj, proj, proj, proj, proj, lb, *consts)


HG_GATE_COLS = 256


def _hg_out_kernel(*refs):
    of_ref, ob_ref, w_ref = refs[0], refs[1], refs[2]
    g_refs, y_ref = refs[3:-1], refs[-1]
    o = of_ref[...].astype(f32) + ob_ref[...].astype(f32)
    g = jnp.concatenate([r[...] for r in g_refs], axis=1).astype(f32)
    rs = lax.rsqrt(jnp.mean(o * o, axis=-1, keepdims=True) + RMS_EPS)
    y_ref[...] = (o * rs * w_ref[...] * _silu(g)).astype(y_ref.dtype)


def _hg_out(o_f, o_b, proj, norm_w):
    s, d = o_f.shape
    tm = 256
    cw = HG_GATE_COLS
    tile = pl.BlockSpec((tm, d), lambda i: (i, 0))
    gates = [pl.BlockSpec((tm, cw), functools.partial(lambda i, u: (i, OFF_G // cw + u), u=u))
             for u in range(d // cw)]
    return pl.pallas_call(
        _hg_out_kernel,
        grid=(s // tm,),
        in_specs=[tile, tile, pl.BlockSpec((1, d), lambda i: (0, 0))] + gates,
        out_specs=tile,
        out_shape=jax.ShapeDtypeStruct((s, d), bf16),
        compiler_params=_params(("parallel",)),
        name="hgrn_out",
    )(o_f, o_b, norm_w.reshape(1, d), *([proj] * len(gates)))


def _post_mix_kernel(y_ref, x_ref, gm_ref, wpost_ref, wpre_ref, sc_ref, sh_ref, wr_ref,
                     x1_ref, hp_ref, lg_ref):
    y = y_ref[...]
    yn = y * lax.rsqrt(jnp.mean(y * y, axis=-1, keepdims=True) + RMS_EPS) * wpost_ref[...]
    x1 = x_ref[...] + gm_ref[...] * yn
    x1_ref[...] = x1
    h = x1 * lax.rsqrt(jnp.mean(x1 * x1, axis=-1, keepdims=True) + RMS_EPS) * wpre_ref[...]
    h = h * (1.0 + sc_ref[...]) + sh_ref[...]
    hhi, hlo = _split2(h)
    whi, wlo = _split2(wr_ref[...])
    lg_ref[...] = _dot(hhi, whi) + _dot(hhi, wlo) + _dot(hlo, whi)
    half = h.shape[1] // 2
    top = pltpu.bitcast(hhi[:, :half].astype(f32), jnp.uint32)
    bot = pltpu.bitcast(hhi[:, half:].astype(f32), jnp.uint32)
    hp_ref[...] = top | (bot >> 16)


def _post_mix(y, x, gm, wpost, wpre, sc, sh, wr_pad):
    s, d = x.shape
    tm = 256
    row = pl.BlockSpec((1, d), lambda i: (0, 0))
    tile = pl.BlockSpec((tm, d), lambda i: (i, 0))
    return pl.pallas_call(
        _post_mix_kernel,
        grid=(s // tm,),
        in_specs=[tile, tile, row, row, row, row, row, pl.BlockSpec((d, LANE), lambda i: (0, 0))],
        out_specs=[tile, pl.BlockSpec((tm, d // 2), lambda i: (i, 0)),
                   pl.BlockSpec((tm, LANE), lambda i: (i, 0))],
        out_shape=[jax.ShapeDtypeStruct((s, d), f32), jax.ShapeDtypeStruct((s, d // 2), jnp.uint32),
                   jax.ShapeDtypeStruct((s, LANE), f32)],
        compiler_params=_params(("parallel",), VMEM_LIMIT),
        name="post_mix",
    )(y, x, gm, wpost.reshape(1, d), wpre.reshape(1, d), sc, sh, wr_pad)


def _select_kernel(lg_ref, utri_ref, idx_ref, gate_ref, pos_ref, sel_ref,
                   aff_ref, cin_ref, *, seq, cap):
    E = N_EXPERTS
    nb = seq // LANE
    lg = lg_ref[...]
    lane = lax.broadcasted_iota(jnp.int32, lg.shape, 1)
    lg = jnp.where(lane < E, lg, -jnp.inf)
    mx = jnp.max(lg, axis=-1, keepdims=True)
    ex = jnp.exp(lg - mx)
    aff = ex / jnp.sum(ex, axis=-1, keepdims=True)
    aff_t = aff.T[:E, :]
    aff_ref[...] = aff_t
    bits = pltpu.bitcast(aff_t, jnp.int32)

    def search(i, thr):
        cand = thr | (jnp.int32(1) << (30 - i))
        cnt = jnp.sum((bits >= cand).astype(jnp.int32), axis=-1, keepdims=True)
        return jnp.where(cnt >= cap, cand, thr)

    thr = lax.fori_loop(0, 31, search, jnp.zeros((E, 1), jnp.int32))
    gt = bits > thr
    eq = bits == thr
    need = cap - jnp.sum(gt.astype(jnp.int32), axis=-1, keepdims=True)
    utri = utri_ref[...]

    def cumsum_excl(mask_f):
        carry = jnp.zeros((E, 1), f32)
        outs = []
        for b in range(nb):
            blk = mask_f[:, b * LANE:(b + 1) * LANE]
            inc = _dot(blk.astype(bf16), utri)
            outs.append(inc - blk + carry)
            carry = carry + inc[:, LANE - 1:LANE]
        return jnp.concatenate(outs, axis=1)

    eq_f = eq.astype(f32)
    rank_eq = cumsum_excl(eq_f)
    sel = gt | (eq & (rank_eq < need.astype(f32)))
    sel_f = sel.astype(f32)
    pos = cumsum_excl(sel_f)
    pos_ref[...] = pos
    sel_ref[...] = sel_f
    cin_ref[...] = jnp.where(sel, pos + 1.0, 0.0)
    pcol = lax.broadcasted_iota(jnp.int32, (cap, LANE), 0).astype(f32)
    lanes = lax.broadcasted_iota(jnp.int32, (cap, LANE), 1)
    idx_m = jnp.zeros((cap, LANE), f32)
    gate_m = jnp.zeros((cap, LANE), f32)
    for e in range(E):
        def blk_body(b, carry):
            cnt_acc, g_acc = carry
            c0 = pl.multiple_of(b * LANE, LANE)
            posr = pos_ref[pl.ds(e, 1), pl.ds(c0, LANE)]
            selr = sel_ref[pl.ds(e, 1), pl.ds(c0, LANE)]
            cinr = cin_ref[pl.ds(e, 1), pl.ds(c0, LANE)]
            affr = aff_ref[pl.ds(e, 1), pl.ds(c0, LANE)]
            incl = posr + selr
            cnt_acc = cnt_acc + jnp.where(incl <= pcol, 1.0, 0.0)
            g_acc = g_acc + jnp.where(cinr == pcol + 1.0, affr, 0.0)
            return cnt_acc, g_acc

        zero = jnp.zeros((cap, LANE), f32)
        cnt_acc, g_acc = lax.fori_loop(0, nb, blk_body, (zero, zero))
        idx_col = jnp.sum(cnt_acc, axis=-1, keepdims=True)
        gate_col = jnp.sum(g_acc, axis=-1, keepdims=True)
        idx_m = jnp.where(lanes == e, idx_col, idx_m)
        gate_m = jnp.where(lanes == e, gate_col, gate_m)
    idx_ref[...] = idx_m.astype(jnp.int32)
    gate_ref[...] = gate_m


def _select(logits, cap):
    s = logits.shape[0]
    t = np.arange(LANE)
    utri = jnp.asarray((t[:, None] <= t[None, :]).astype(np.float32), bf16)
    full = lambda shape: pl.BlockSpec(shape, lambda: tuple(0 for _ in shape))
    return pl.pallas_call(
        functools.partial(_select_kernel, seq=s, cap=cap),
        in_specs=[full((s, LANE)), full((LANE, LANE))],
        out_specs=[full((cap, LANE)), full((cap, LANE)), full((N_EXPERTS, s)), full((N_EXPERTS, s))],
        out_shape=[jax.ShapeDtypeStruct((cap, LANE), jnp.int32), jax.ShapeDtypeStruct((cap, LANE), f32),
                   jax.ShapeDtypeStruct((N_EXPERTS, s), f32), jax.ShapeDtypeStruct((N_EXPERTS, s), f32)],
        scratch_shapes=[pltpu.VMEM((N_EXPERTS, s), f32), pltpu.VMEM((N_EXPERTS, s), f32)],
        compiler_params=_params(None, VMEM_LIMIT),
        name="ec_select",
    )(logits, utri)


FF_TILE = 256
DOWN_TILE = 512


def _expert_kernel(idx_ref, hp_ref, wg_ref, wu_ref, wd_ref, gate_ref, y_ref,
                   xbuf_ref, xa_ref, xb_ref, hid_ref, sem, *, cap):
    e = pl.program_id(0)
    s = pl.program_id(1)
    n_ff = D_FF_EXPERT // FF_TILE
    half = D_MODEL // 2

    def gather(ex):
        def start(p, c):
            pltpu.make_async_copy(hp_ref.at[pl.ds(idx_ref[ex, p], 1), :],
                                  xbuf_ref.at[pl.ds(p, 1), :], sem).start()
            return c

        lax.fori_loop(0, cap, start, 0)

    @pl.when((s == 0) & (e == 0))
    def _():
        gather(0)

    @pl.when(s == 0)
    def _():
        pltpu.make_async_copy(hp_ref.at[pl.ds(0, cap), :], xbuf_ref, sem).wait()
        u = xbuf_ref[...]
        xa_ref[...] = pltpu.bitcast(u & jnp.uint32(0xFFFF0000), f32).astype(bf16)
        xb_ref[...] = pltpu.bitcast(u << 16, f32).astype(bf16)

    @pl.when((s == 0) & (e + 1 < N_EXPERTS))
    def _():
        gather(e + 1)

    @pl.when(s < n_ff)
    def _():
        wg = wg_ref[0].astype(bf16)
        wu = wu_ref[0].astype(bf16)
        xa = xa_ref[...]
        xb = xb_ref[...]
        hg = _dot(xa, wg[:half]) + _dot(xb, wg[half:])
        hu = _dot(xa, wu[:half]) + _dot(xb, wu[half:])
        c0 = pl.multiple_of(s * FF_TILE, FF_TILE)
        hid_ref[:, pl.ds(c0, FF_TILE)] = (_silu(hg) * hu).astype(bf16)

    @pl.when(s >= n_ff)
    def _():
        y = _dot(hid_ref[...], wd_ref[0].astype(bf16))
        y_ref[0] = (y * gate_ref[0]).astype(y_ref.dtype)


def _experts(idx, hp, w_gate, w_up, w_down, gate, cap):
    n_ff = D_FF_EXPERT // FF_TILE
    n_dn = D_MODEL // DOWN_TILE
    ffj = lambda s: jnp.minimum(s, n_ff - 1)
    dnj = lambda s: jnp.maximum(s - n_ff, 0)
    grid_spec = pltpu.PrefetchScalarGridSpec(
        num_scalar_prefetch=1,
        grid=(N_EXPERTS, n_ff + n_dn),
        in_specs=[pl.BlockSpec(memory_space=pl.ANY),
                  pl.BlockSpec((1, D_MODEL, FF_TILE), lambda e, s, idx: (e, 0, ffj(s))),
                  pl.BlockSpec((1, D_MODEL, FF_TILE), lambda e, s, idx: (e, 0, ffj(s))),
                  pl.BlockSpec((1, D_FF_EXPERT, DOWN_TILE), lambda e, s, idx: (e, 0, dnj(s))),
                  pl.BlockSpec((1, cap, 1), lambda e, s, idx: (e, 0, 0))],
        out_specs=pl.BlockSpec((1, cap, DOWN_TILE), lambda e, s, idx: (e, 0, dnj(s))),
        scratch_shapes=[pltpu.VMEM((cap, D_MODEL // 2), jnp.uint32),
                        pltpu.VMEM((cap, D_MODEL // 2), bf16),
                        pltpu.VMEM((cap, D_MODEL // 2), bf16),
                        pltpu.VMEM((cap, D_FF_EXPERT), bf16),
                        pltpu.SemaphoreType.DMA(())],
    )
    return pl.pallas_call(
        functools.partial(_expert_kernel, cap=cap),
        grid_spec=grid_spec,
        out_shape=jax.ShapeDtypeStruct((N_EXPERTS, cap, D_MODEL), bf16),
        compiler_params=_params(("arbitrary", "arbitrary"), VMEM_LIMIT),
        name="ec_experts",
    )(idx, hp, w_gate, w_up, w_down, gate)


COMBINE_ALIGN = 16


def _combine_kernel(off_ref, y_ref, pos_ref, sel_ref, x1_ref, gf_ref, w_ref, o_ref,
                    buf_ref, acc_ref, sems, *, tt):
    i = pl.program_id(0)
    rows = tt + COMBINE_ALIGN

    def chunk_copy(e, slot):
        start = pl.multiple_of(off_ref[e, i], COMBINE_ALIGN)
        return pltpu.make_async_copy(y_ref.at[e, pl.ds(start, rows), :],
                                     buf_ref.at[slot], sems.at[slot])

    chunk_copy(0, 0).start()
    jrow = lax.broadcasted_iota(jnp.int32, (rows, tt), 0).astype(f32)
    for e in range(N_EXPERTS):
        slot = e % 2
        if e + 1 < N_EXPERTS:
            chunk_copy(e + 1, 1 - slot).start()
        rel = pos_ref[e:e + 1, :] - off_ref[e, i].astype(f32)
        et = jnp.where((sel_ref[e:e + 1, :] > 0.0) & (rel == jrow), 1.0, 0.0)
        chunk_copy(e, slot).wait()
        part = lax.dot_general(et.astype(bf16), buf_ref[slot], (((0,), (0,)), ((), ())),
                               preferred_element_type=f32)
        if e == 0:
            acc_ref[...] = part
        else:
            acc_ref[...] += part
    y = acc_ref[...]
    yn = y * lax.rsqrt(jnp.mean(y * y, axis=-1, keepdims=True) + RMS_EPS) * w_ref[...]
    o_ref[...] = x1_ref[...] + gf_ref[...] * yn


def _combine(off, yexp, pos, sel, x1, gf, w_post):
    s, d = x1.shape
    tt = pos.shape[1] // off.shape[1]
    row = pl.BlockSpec((1, d), lambda i, off: (0, 0))
    grid_spec = pltpu.PrefetchScalarGridSpec(
        num_scalar_prefetch=1,
        grid=(s // tt,),
        in_specs=[pl.BlockSpec(memory_space=pl.ANY),
                  pl.BlockSpec((N_EXPERTS, tt), lambda i, off: (0, i)),
                  pl.BlockSpec((N_EXPERTS, tt), lambda i, off: (0, i)),
                  pl.BlockSpec((tt, d), lambda i, off: (i, 0)), row, row],
        out_specs=pl.BlockSpec((tt, d), lambda i, off: (i, 0)),
        scratch_shapes=[pltpu.VMEM((2, tt + COMBINE_ALIGN, d), bf16), pltpu.VMEM((tt, d), f32),
                        pltpu.SemaphoreType.DMA((2,))],
    )
    return pl.pallas_call(
        functools.partial(_combine_kernel, tt=tt),
        grid_spec=grid_spec,
        out_shape=jax.ShapeDtypeStruct((s, d), f32),
        compiler_params=_params(("arbitrary",), VMEM_LIMIT),
        name="ec_combine",
    )(off, yexp, pos, sel, x1, gf, w_post.reshape(1, d))


GATE_SUB = 256


def _gate_extras(proj, off, tm, tn):
    n_sub = tn // GATE_SUB
    return [(proj, (tm, GATE_SUB), functools.partial(
        lambda m, j, u: (m, off // GATE_SUB + j * n_sub + u), u=u)) for u in range(n_sub)]


def _merge_a(acc, *gate_refs):
    ga = jnp.concatenate([r[...] for r in gate_refs], axis=1).astype(f32)
    return _sigmoid(ga) * acc


def _merge_b(acc, part_ref, *gate_refs):
    gb = jnp.concatenate([r[...] for r in gate_refs], axis=1).astype(f32)
    return part_ref[...] + _sigmoid(gb) * acc


def _layer(x, mod, lb, norm_pre_mix, norm_post_mix, norm_pre_ffn, norm_post_ffn,
           w_in, conv_w, conv_b, dt_bias, a_log, d_skip, ssd_norm_w, hg_norm_w,
           w_ssd_out, w_hg_out, w_mix_out, w_router, w_gate, w_up, w_down):
    s, d = x.shape
    sh_m, sc_m, g_m, sh_f, sc_f, g_f = [mod[:, i * d:(i + 1) * d] for i in range(6)]

    h = _prenorm(x, norm_pre_mix, sc_m, sh_m)
    tm8, tm16 = s // 8, s // 16
    proj = _wsmm(h, w_in, tm=tm16, tn=1280, out_dtype=bf16, name="in_proj")

    xact = _conv(proj, conv_w, conv_b)
    tri_f, tri_b, sel_a, sel_w, ea, ew = _ssd_consts()
    cb = lambda a: jnp.asarray(a, bf16)
    paf, pwf, arf, pab, pwb, arb = _ssd_prep(proj, dt_bias, a_log,
                                             (cb(tri_f), cb(tri_b), cb(sel_a), cb(sel_w)))
    y_f = _ssd(xact, paf, pwf, arf, cb(ea), cb(ew), rev=False)
    dskip = jnp.repeat(d_skip, SSD_HEAD_DIM).reshape(1, SSD_D_INNER)
    y_ssd = _ssd(xact, pab, pwb, arb, cb(ea), cb(ew), rev=True,
                 final_args=(y_f, proj, ssd_norm_w.reshape(1, SSD_D_INNER), dskip))

    hg_consts = [tuple((cb(tri), cb(tsm), jnp.asarray(msk))) for tri, tsm, msk in
                 (_hg_consts(False), _hg_consts(True))]
    o_f, o_b = _hgrn(proj, lb, hg_consts[0], hg_consts[1])
    y_hg = _hg_out(o_f, o_b, proj, hg_norm_w)

    part = _wsmm(y_ssd, w_ssd_out, tm=tm16, tn=512, out_dtype=f32,
                 extras=_gate_extras(proj, OFF_GA, tm16, 512),
                 epilogue=_merge_a, name="ssd_out_proj")
    merged = _wsmm(y_hg, w_hg_out, tm=tm8, tn=512, out_dtype=bf16,
                   extras=[(part, (tm8, 512), lambda m, j: (m, j))]
                   + _gate_extras(proj, OFF_GB, tm8, 512),
                   epilogue=_merge_b, name="hg_out_proj")
    y_mix = _wsmm(merged, w_mix_out, tm=tm8, tn=512, out_dtype=f32, name="mix_out_proj")

    wr_pad = jnp.pad(w_router, ((0, 0), (0, LANE - N_EXPERTS)))
    x1, hp, logits = _post_mix(y_mix, x, g_m, norm_post_mix, norm_pre_ffn, sc_f, sh_f, wr_pad)
    cap = 2 * s // N_EXPERTS
    idx_m, gate_m, pos, sel = _select(logits, cap)
    idx = idx_m[:, :N_EXPERTS].T
    gate = gate_m[:, :N_EXPERTS].T.reshape(N_EXPERTS, cap, 1)
    yexp = _experts(idx, hp, w_gate, w_up, w_down, gate, cap)
    tt = min(128, cap // 2)
    first_pos = pos[:, ::tt].astype(jnp.int32)
    off = jnp.minimum(first_pos // COMBINE_ALIGN * COMBINE_ALIGN, cap - (tt + COMBINE_ALIGN))
    return _combine(off, yexp, pos, sel, x1, g_f, norm_post_ffn)


def kernel(x, c, w_ada, b_ada, norm_pre_mix, norm_post_mix, norm_pre_ffn, norm_post_ffn, w_in, conv_w, conv_b, dt_bias_fwd, dt_bias_bwd, a_log_fwd, a_log_bwd, d_skip, ssd_norm_w, hg_lower_bound, hg_norm_w, w_ssd_out, w_hg_out, w_mix_out, w_router, w_gate, w_up, w_down):
    depth = w_ada.shape[0]
    lower_bounds = jnp.cumsum(jax.nn.softmax(hg_lower_bound.astype(f32), axis=0), axis=0)
    outs = []
    for bi in range(x.shape[0]):
        xb = x[bi]
        cb_ = c[bi:bi + 1]
        for l in range(depth):
            mod = _ada(cb_, w_ada[l], b_ada[l])
            dt_bias = jnp.concatenate([dt_bias_fwd[l], dt_bias_bwd[l]]).reshape(1, -1)
            a_log = jnp.concatenate([a_log_fwd[l], a_log_bwd[l]]).reshape(1, -1)
            xb = _layer(xb, mod, lower_bounds[l].reshape(1, -1), norm_pre_mix[l], norm_post_mix[l],
                        norm_pre_ffn[l], norm_post_ffn[l], w_in[l], conv_w[l], conv_b[l],
                        dt_bias, a_log, d_skip[l], ssd_norm_w[l], hg_norm_w[l],
                        w_ssd_out[l], w_hg_out[l], w_mix_out[l], w_router[l],
                        w_gate[l], w_up[l], w_down[l])
        outs.append(xb)
    return jnp.stack(outs, axis=0)
```

```python
import functools

import numpy as np
import jax
import jax.numpy as jnp
from jax import lax
from jax.experimental import pallas as pl
from jax.experimental.pallas import tpu as pltpu

f32 = jnp.float32
bf16 = jnp.bfloat16

D_MODEL = 4096
SSD_D_INNER = 8192
SSD_HEADS = 128
SSD_HEAD_DIM = 64
SSD_D_STATE = 128
SSD_GROUPS = 8
SSD_HEADS_PER_GROUP = 16
SSD_CONV = 5
SSD_XBC = SSD_D_INNER + 2 * SSD_GROUPS * SSD_D_STATE
HG_HEADS = 32
HG_HEAD_DIM = 128
N_EXPERTS = 16
D_FF_EXPERT = 2048
RMS_EPS = 1e-6

OFF_Z = 0
OFF_XBC = OFF_Z + SSD_D_INNER
OFF_DT = OFF_XBC + SSD_XBC
OFF_Q = OFF_DT + 2 * SSD_HEADS
OFF_FF = OFF_Q + D_MODEL
OFF_FB = OFF_FF + D_MODEL
OFF_I = OFF_FB + D_MODEL
OFF_G = OFF_I + D_MODEL
OFF_GA = OFF_G + D_MODEL
OFF_GB = OFF_GA + D_MODEL
D_IN_PROJ = OFF_GB + D_MODEL

LANE = 128
CHUNK = 128
VMEM_LIMIT = 56 * 1024 * 1024


def _params(sem, vmem=None):
    return pltpu.CompilerParams(dimension_semantics=sem, vmem_limit_bytes=vmem)


def _sigmoid(x):
    return 1.0 / (1.0 + jnp.exp(-x))


def _silu(x):
    return x * _sigmoid(x)


def _softplus(x):
    return jnp.maximum(x, 0.0) + jnp.log(1.0 + jnp.exp(-jnp.abs(x)))


def _split2(x):
    hi = x.astype(bf16)
    lo = (x - hi.astype(f32)).astype(bf16)
    return hi, lo


def _split3(x):
    hi = x.astype(bf16)
    r = x - hi.astype(f32)
    mid = r.astype(bf16)
    lo = (r - mid.astype(f32)).astype(bf16)
    return hi, mid, lo


def _dot(a, b):
    return jnp.dot(a, b, preferred_element_type=f32)


def _dot_nt(a, b):
    return lax.dot_general(a, b, (((1,), (1,)), ((), ())), preferred_element_type=f32)


def _ada_kernel(c_ref, w_ref, b_ref, o_ref):
    c = c_ref[...]
    ca = jnp.broadcast_to(_silu(c), (8, c.shape[1]))
    chi, clo = _split2(ca)
    whi, wlo = _split2(w_ref[...])
    acc = _dot(chi, whi) + _dot(chi, wlo) + _dot(clo, whi)
    o_ref[...] = acc[0:1, :] + b_ref[...]


def _ada(c, w, b):
    d, n = w.shape
    tn = 512
    return pl.pallas_call(
        _ada_kernel,
        grid=(n // tn,),
        in_specs=[pl.BlockSpec((1, d), lambda j: (0, 0)),
                  pl.BlockSpec((d, tn), lambda j: (0, j)),
                  pl.BlockSpec((1, tn), lambda j: (0, j))],
        out_specs=pl.BlockSpec((1, tn), lambda j: (0, j)),
        out_shape=jax.ShapeDtypeStruct((1, n), f32),
        compiler_params=_params(("parallel",), VMEM_LIMIT),
        name="adaln",
    )(c, w, b.reshape(1, n))


def _prenorm_kernel(x_ref, w_ref, sc_ref, sh_ref, o_ref):
    x = x_ref[...]
    r = lax.rsqrt(jnp.mean(x * x, axis=-1, keepdims=True) + RMS_EPS)
    h = (x * r * w_ref[...]) * (1.0 + sc_ref[...]) + sh_ref[...]
    o_ref[...] = h.astype(o_ref.dtype)


def _prenorm(x, w, sc, sh):
    s, d = x.shape
    tm = 256
    row = pl.BlockSpec((1, d), lambda i: (0, 0))
    return pl.pallas_call(
        _prenorm_kernel,
        grid=(s // tm,),
        in_specs=[pl.BlockSpec((tm, d), lambda i: (i, 0)), row, row, row],
        out_specs=pl.BlockSpec((tm, d), lambda i: (i, 0)),
        out_shape=jax.ShapeDtypeStruct((s, d), bf16),
        compiler_params=_params(("parallel",)),
        name="prenorm",
    )(x, w.reshape(1, d), sc, sh)


def _wsmm_kernel(*refs, n_extra, nn, epilogue):
    a_ref, w_ref = refs[0], refs[1]
    extras = refs[2:2 + n_extra]
    o_ref = refs[2 + n_extra]
    wbuf_ref = refs[3 + n_extra]
    n = pl.program_id(0)
    m = pl.program_id(1)
    kc = w_ref.shape[0]

    @pl.when(n < nn)
    def _():
        r0 = pl.multiple_of(m * kc, kc)
        wbuf_ref[n % 2, pl.ds(r0, kc), :] = w_ref[...].astype(bf16)

    @pl.when(n == 0)
    def _():
        o_ref[...] = jnp.zeros_like(o_ref)

    @pl.when(n > 0)
    def _():
        acc = _dot(a_ref[...], wbuf_ref[(n - 1) % 2])
        o_ref[...] = epilogue(acc, *extras).astype(o_ref.dtype)


def _wsmm(a, w, *, tm, tn, out_dtype, extras=(), epilogue=None, name="matmul"):
    m, kdim = a.shape
    n = w.shape[1]
    nm, nn = m // tm, n // tn
    kc = kdim // nm
    if epilogue is None:
        epilogue = lambda acc: acc
    jn = lambda nidx: jnp.maximum(nidx - 1, 0)
    extra_arrays = [e[0] for e in extras]
    extra_specs = [pl.BlockSpec(e[1], functools.partial(lambda nidx, midx, f: f(midx, jn(nidx)), f=e[2]))
                   for e in extras]
    return pl.pallas_call(
        functools.partial(_wsmm_kernel, n_extra=len(extras), nn=nn, epilogue=epilogue),
        grid=(nn + 1, nm),
        in_specs=[pl.BlockSpec((tm, kdim), lambda nidx, midx: (midx, 0)),
                  pl.BlockSpec((kc, tn), lambda nidx, midx: (midx, jnp.minimum(nidx, nn - 1)))] + extra_specs,
        out_specs=pl.BlockSpec((tm, tn), lambda nidx, midx: (jnp.where(nidx == 0, 0, midx), jn(nidx))),
        out_shape=jax.ShapeDtypeStruct((m, n), out_dtype),
        scratch_shapes=[pltpu.VMEM((2, kdim, tn), bf16)],
        compiler_params=_params(("arbitrary", "arbitrary"), VMEM_LIMIT),
        name=name,
    )(a, w, *extra_arrays)


CONV_HALO = 16


def _conv_kernel(x_ref, w_ref, b_ref, o_ref, *, seq, tile):
    w = w_ref[...]
    b = b_ref[...]
    nt = seq // tile
    ext_rows = tile + 2 * CONV_HALO

    def body(i, carry):
        r0 = pl.multiple_of(i * tile, tile)
        cur = x_ref[pl.ds(r0, tile), :].astype(f32)
        p0 = pl.multiple_of(jnp.maximum(r0 - CONV_HALO, 0), CONV_HALO)
        n0 = pl.multiple_of(jnp.minimum(r0 + tile, seq - CONV_HALO), CONV_HALO)
        prev = jnp.where(i > 0, x_ref[pl.ds(p0, CONV_HALO), :].astype(f32), 0.0)
        nxt = jnp.where(i < nt - 1, x_ref[pl.ds(n0, CONV_HALO), :].astype(f32), 0.0)
        ext = jnp.concatenate([prev, cur, nxt], axis=0)
        acc = b + w[2:3, :] * cur
        for k in (0, 1, 3, 4):
            d = k - SSD_CONV // 2
            shifted = pltpu.roll(ext, (-d) % ext_rows, axis=0)[CONV_HALO:CONV_HALO + tile, :]
            acc = acc + w[k:k + 1, :] * shifted
        o_ref[pl.ds(r0, tile), :] = _silu(acc).astype(o_ref.dtype)
        return carry

    lax.fori_loop(0, nt, body, 0)


def _conv(proj, conv_w, conv_b):
    s = proj.shape[0]
    cw = 256
    off = OFF_XBC // cw
    tile = min(512, s)
    return pl.pallas_call(
        functools.partial(_conv_kernel, seq=s, tile=tile),
        grid=(SSD_XBC // cw,),
        in_specs=[pl.BlockSpec((s, cw), lambda j: (0, off + j)),
                  pl.BlockSpec((SSD_CONV, cw), lambda j: (0, j)),
                  pl.BlockSpec((1, cw), lambda j: (0, j))],
        out_specs=pl.BlockSpec((s, cw), lambda j: (0, j)),
        out_shape=jax.ShapeDtypeStruct((s, SSD_XBC), bf16),
        compiler_params=_params(("parallel",), VMEM_LIMIT),
        name="conv_silu",
    )(proj, conv_w, conv_b.reshape(1, SSD_XBC))


def _ssd_consts():
    L = CHUNK
    t = np.arange(L)
    tri_f = (t[None, :] <= t[:, None]).astype(np.float32)
    tri_b = (t[None, :] >= t[:, None]).astype(np.float32)
    h = np.arange(SSD_HEADS)
    g, r = h // SSD_HEADS_PER_GROUP, h % SSD_HEADS_PER_GROUP
    sel_a = np.zeros((3 * SSD_HEADS, SSD_GROUPS * LANE), np.float32)
    for j in range(3):
        sel_a[j * SSD_HEADS + h, g * LANE + j * SSD_HEADS_PER_GROUP + r] = 1.0
    sel_w = np.zeros((6 * SSD_HEADS, SSD_GROUPS * LANE), np.float32)
    for j in range(6):
        sel_w[j * SSD_HEADS + h, g * LANE + j * SSD_HEADS_PER_GROUP + r] = 1.0
    ea = np.zeros((LANE, SSD_HEADS_PER_GROUP * L), np.float32)
    for j in range(3):
        for rr in range(SSD_HEADS_PER_GROUP):
            ea[j * SSD_HEADS_PER_GROUP + rr, rr * L:(rr + 1) * L] = 1.0
    gw = SSD_HEADS_PER_GROUP * SSD_HEAD_DIM
    ew = np.zeros((LANE, 3 * gw), np.float32)
    for j in range(6):
        for rr in range(SSD_HEADS_PER_GROUP):
            ew[j * SSD_HEADS_PER_GROUP + rr,
               (j // 2) * gw + rr * SSD_HEAD_DIM:(j // 2) * gw + (rr + 1) * SSD_HEAD_DIM] = 1.0
    return tri_f, tri_b, sel_a, sel_w, ea, ew


def _ssd_prep_kernel(dt_ref, bias_ref, alog_ref, trif_ref, trib_ref, sela_ref, selw_ref,
                     paf_ref, pwf_ref, arf_ref, pab_ref, pwb_ref, arb_ref):
    raw = dt_ref[...].astype(f32)
    L = raw.shape[0]
    outs = ((paf_ref, pwf_ref, arf_ref, trif_ref, L - 1), (pab_ref, pwb_ref, arb_ref, trib_ref, 0))
    for d, (pa_ref, pw_ref, ar_ref, tri_ref, tot_row) in enumerate(outs):
        sl = slice(d * SSD_HEADS, (d + 1) * SSD_HEADS)
        dt = _softplus(raw[:, sl] + bias_ref[:, sl])
        adt = dt * (-jnp.exp(alog_ref[:, sl]))
        h3 = jnp.concatenate(_split3(adt), axis=1)
        a3 = _dot(tri_ref[...], h3)
        a = (a3[:, :SSD_HEADS] + a3[:, SSD_HEADS:2 * SSD_HEADS]) + a3[:, 2 * SSD_HEADS:]
        atot = a[tot_row:tot_row + 1, :]
        wgt = jnp.exp(atot - a) * dt
        ea = jnp.exp(a)
        ar_ref[...] = a.T
        pa_ref[...] = _dot(jnp.concatenate(_split3(a), axis=1), sela_ref[...]).astype(bf16)
        cols = _split2(wgt) + _split2(ea) + _split2(dt)
        pw_ref[...] = _dot(jnp.concatenate(cols, axis=1), selw_ref[...]).astype(bf16)


def _ssd_prep(proj, dt_bias, a_log, consts):
    s = proj.shape[0]
    L = CHUNK
    tri_f, tri_b, sel_a, sel_w = consts
    gl = SSD_GROUPS * LANE
    const = lambda shape: pl.BlockSpec(shape, lambda c: (0, 0))
    packed = pl.BlockSpec((L, gl), lambda c: (c, 0))
    rowf = pl.BlockSpec((SSD_HEADS, L), lambda c: (0, c))
    return pl.pallas_call(
        _ssd_prep_kernel,
        grid=(s // L,),
        in_specs=[pl.BlockSpec((L, 2 * SSD_HEADS), lambda c: (c, OFF_DT // (2 * SSD_HEADS))),
                  const((1, 2 * SSD_HEADS)), const((1, 2 * SSD_HEADS)),
                  const((L, L)), const((L, L)),
                  const(sel_a.shape), const(sel_w.shape)],
        out_specs=[packed, packed, rowf, packed, packed, rowf],
        out_shape=[jax.ShapeDtypeStruct((s, gl), bf16), jax.ShapeDtypeStruct((s, gl), bf16),
                   jax.ShapeDtypeStruct((SSD_HEADS, s), f32)] * 2,
        compiler_params=_params(("parallel",)),
        name="ssd_prep",
    )(proj, dt_bias, a_log, tri_f, tri_b, sel_a, sel_w)


def _ssd_kernel(*refs, rev, final):
    if final:
        (x_ref, b_ref, c_ref, pa_ref, pw_ref, ar_ref, ea_ref, ew_ref,
         yprev_ref, z_ref, nw_ref, dsk_ref, o_ref, st_ref) = refs
    else:
        x_ref, b_ref, c_ref, pa_ref, pw_ref, ar_ref, ea_ref, ew_ref, o_ref, st_ref = refs
    L = CHUNK
    gw = SSD_HEADS_PER_GROUP * SSD_HEAD_DIM
    npair = SSD_HEADS_PER_GROUP // 2

    @pl.when(pl.program_id(1) == 0)
    def _():
        st_ref[...] = jnp.zeros_like(st_ref)

    x = x_ref[...]
    bm = b_ref[...]
    cm = c_ref[...]
    aexp = _dot(pa_ref[...], ea_ref[...])
    wed = _dot(pw_ref[...], ew_ref[...])
    wexp, eexp, dtexp = wed[:, :gw], wed[:, gw:2 * gw], wed[:, 2 * gw:]
    xf = x.astype(f32)
    xdt = (xf * dtexp).astype(bf16)
    xw = (xf * wexp).astype(bf16)
    tot_row = 0 if rev else L - 1
    dec_tot = eexp[tot_row:tot_row + 1, :]
    bt = bm.astype(f32).T.astype(bf16)
    cb = _dot(cm, bt)
    ti = lax.broadcasted_iota(jnp.int32, (L, L), 0)
    si = lax.broadcasted_iota(jnp.int32, (L, L), 1)
    keep = (si >= ti) if rev else (si <= ti)
    cbm = jnp.where(keep, cb, 0.0)
    lane = lax.broadcasted_iota(jnp.int32, (L, LANE), 1)
    first = lane < SSD_HEAD_DIM
    ys = []
    for j in range(npair):
        ds_ = []
        for r in (2 * j, 2 * j + 1):
            diff = aexp[:, r * L:(r + 1) * L] - ar_ref[r:r + 1, :]
            ds_.append((jnp.exp(jnp.minimum(diff, 0.0)) * cbm).astype(bf16))
        mp = jnp.concatenate(ds_, axis=1)
        xp = xdt[:, j * LANE:(j + 1) * LANE]
        zero = jnp.zeros_like(xp)
        xbd = jnp.concatenate([jnp.where(first, xp, zero), jnp.where(first, zero, xp)], axis=0)
        st = st_ref[j]
        y = _dot(mp, xbd) + _dot(cm, st.astype(bf16)) * eexp[:, j * LANE:(j + 1) * LANE]
        ys.append(y)
        st_ref[j] = st * dec_tot[:, j * LANE:(j + 1) * LANE] + _dot(bt, xw[:, j * LANE:(j + 1) * LANE])
    y = jnp.concatenate(ys, axis=1)
    if final:
        y = y + yprev_ref[...].astype(f32) + dsk_ref[...] * xf
        y = y * _silu(z_ref[...].astype(f32))
        y = y * lax.rsqrt(jnp.mean(y * y, axis=-1, keepdims=True) + RMS_EPS) * nw_ref[...]
    o_ref[...] = y.astype(o_ref.dtype)


def _ssd(xact, pa, pw, arow, ea, ew, *, rev, final_args=None):
    s = xact.shape[0]
    L = CHUNK
    nc = s // L
    gw = SSD_HEADS_PER_GROUP * SSD_HEAD_DIM
    cc = (lambda c: nc - 1 - c) if rev else (lambda c: c)
    nxg = SSD_D_INNER // LANE
    in_specs = [pl.BlockSpec((L, gw), lambda g, c: (cc(c), g)),
                pl.BlockSpec((L, LANE), lambda g, c: (cc(c), nxg + g)),
                pl.BlockSpec((L, LANE), lambda g, c: (cc(c), nxg + SSD_GROUPS + g)),
                pl.BlockSpec((L, LANE), lambda g, c: (cc(c), g)),
                pl.BlockSpec((L, LANE), lambda g, c: (cc(c), g)),
                pl.BlockSpec((SSD_HEADS_PER_GROUP, L), lambda g, c: (g, cc(c))),
                pl.BlockSpec(ea.shape, lambda g, c: (0, 0)),
                pl.BlockSpec(ew.shape, lambda g, c: (0, 0))]
    args = [xact, xact, xact, pa, pw, arow, ea, ew]
    final = final_args is not None
    if final:
        yprev, proj, norm_w, dskip = final_args
        in_specs += [pl.BlockSpec((L, gw), lambda g, c: (cc(c), g)),
                     pl.BlockSpec((L, gw), lambda g, c: (cc(c), OFF_Z // gw + g)),
                     pl.BlockSpec((1, gw), lambda g, c: (0, g)),
                     pl.BlockSpec((1, gw), lambda g, c: (0, g))]
        args += [yprev, proj, norm_w, dskip]
    return pl.pallas_call(
        functools.partial(_ssd_kernel, rev=rev, final=final),
        grid=(SSD_GROUPS, nc),
        in_specs=in_specs,
        out_specs=pl.BlockSpec((L, gw), lambda g, c: (cc(c), g)),
        out_shape=jax.ShapeDtypeStruct((s, SSD_D_INNER), bf16),
        scratch_shapes=[pltpu.VMEM((SSD_HEADS_PER_GROUP // 2, SSD_D_STATE, LANE), f32)],
        compiler_params=_params(("parallel", "arbitrary")),
        name="ssd_bwd" if rev else "ssd_fwd",
    )(*args)


HG_LEVELS = 8
HG_MM_LEVELS = 2
HG_PAIR = 2
HG_BLOCKS = 2


def _hg_level_ref_row(lv, blk, rev):
    b = 1 << (lv - 1)
    return blk * 2 * b + (b if rev else b - 1)


def _hg_consts(rev):
    C = CHUNK
    t = np.arange(C)
    if rev:
        tri = (t[None, :] >= t[:, None]).astype(np.float32)
    else:
        tri = (t[None, :] <= t[:, None]).astype(np.float32)
    tmats = []
    masks = [(t[:, None] == t[None, :]).astype(np.float32)]
    for lv in range(1, HG_LEVELS):
        b = 1 << (lv - 1)
        blk = t // (2 * b)
        upper = (t % (2 * b)) >= b
        if rev:
            mask = (blk[:, None] == blk[None, :]) & (~upper)[:, None] & upper[None, :]
        else:
            mask = (blk[:, None] == blk[None, :]) & upper[:, None] & (~upper)[None, :]
        if lv <= HG_MM_LEVELS:
            tmats.append(tri - tri[_hg_level_ref_row(lv, blk, rev)])
        masks.append(mask.astype(np.float32))
    return tri, np.concatenate(tmats, axis=0), np.stack(masks, axis=0)


def _hg_chain(qb, fb, vb, lb, tri_ref, tsm_ref, m_ref, g_ref, st_ref, rev):
    C = CHUNK
    qf = qb.astype(f32)
    fr = fb.astype(f32)
    sg = _sigmoid(fr)
    lf = jnp.log2(lb + (1.0 - lb) * sg)
    kk = (1.0 - lb) * (1.0 - sg)
    kb = kk.astype(bf16)
    h3 = _split3(lf)
    g3 = _dot(tri_ref[...], jnp.concatenate(h3, axis=1))
    g = (g3[:, :LANE] + g3[:, LANE:2 * LANE]) + g3[:, 2 * LANE:]
    e2 = _dot(tsm_ref[...], jnp.concatenate(h3[:2], axis=1))
    e_small = e2[:, :LANE] + e2[:, LANE:]
    g_ref[...] = g
    sc = _dot_nt(qb, kb) * m_ref[0]
    for lv in range(1, HG_LEVELS):
        if lv <= HG_MM_LEVELS:
            e = e_small[(lv - 1) * C:lv * C, :]
        else:
            b = 1 << (lv - 1)
            rows = []
            for blk in range(C // (2 * b)):
                m = _hg_level_ref_row(lv, blk, rev)
                rows.append(jnp.broadcast_to(g_ref[m:m + 1, :], (2 * b, LANE)))
            e = g - jnp.concatenate(rows, axis=0)
        xd = jnp.exp2(-jnp.abs(e)).astype(bf16)
        sc = sc + _dot_nt(qb * xd, kb * xd) * m_ref[lv]
    o = _dot(sc.astype(bf16), vb)
    st = st_ref[...]
    o = o + _dot_nt((qf * jnp.exp2(g)).astype(bf16), st.astype(bf16))
    tot_row = 0 if rev else C - 1
    gtot = g[tot_row:tot_row + 1, :]
    kdec = (kk * jnp.exp2(gtot - g)).astype(bf16)
    vt = vb.astype(f32).T.astype(bf16)
    st_ref[...] = st * jnp.exp2(gtot) + _dot(vt, kdec)
    return o


def _hg_kernel(*refs):
    n_in = 7 * HG_BLOCKS
    data = refs[:n_in]
    trif_ref, tsmf_ref, mf_ref, trib_ref, tsmb_ref, mb_ref = refs[n_in:n_in + 6]
    outs = refs[n_in + 6:n_in + 6 + 2 * HG_BLOCKS]
    st_ref, g_ref = refs[-2:]

    @pl.when(pl.program_id(1) == 0)
    def _():
        st_ref[...] = jnp.zeros_like(st_ref)

    k = 0
    for u in range(HG_BLOCKS):
        qf_ref, ff_ref, vf_ref, qb_ref, fb_ref, vb_ref, lb_ref = data[7 * u:7 * u + 7]
        of_ref, ob_ref = outs[2 * u:2 * u + 2]
        dirs = ((qf_ref, ff_ref, vf_ref, trif_ref, tsmf_ref, mf_ref, of_ref, False),
                (qb_ref, fb_ref, vb_ref, trib_ref, tsmb_ref, mb_ref, ob_ref, True))
        for q_ref, f_ref, v_ref, tri_ref, tsm_ref, m_ref, o_ref, rev in dirs:
            for hh in range(HG_PAIR):
                sl = slice(hh * LANE, (hh + 1) * LANE)
                o = _hg_chain(q_ref[:, sl], f_ref[:, sl], v_ref[:, sl], lb_ref[:, sl],
                              tri_ref, tsm_ref, m_ref, g_ref.at[k], st_ref.at[k], rev)
                o_ref[:, sl] = o.astype(o_ref.dtype)
                k += 1


def _hgrn(proj, lb, consts_f, consts_b):
    s = proj.shape[0]
    C = CHUNK
    nc = s // C
    w = HG_PAIR * LANE
    nb = HG_BLOCKS
    const = lambda a: pl.BlockSpec(a.shape, lambda h, c: tuple(0 for _ in a.shape))
    consts = list(consts_f) + list(consts_b)
    in_specs, args, out_specs = [], [], []
    for u in range(nb):
        fwd = lambda off, u=u: pl.BlockSpec((C, w), lambda h, c: (c, off // w + nb * h + u))
        bwd = lambda off, u=u: pl.BlockSpec((C, w), lambda h, c: (nc - 1 - c, off // w + nb * h + u))
        in_specs += [fwd(OFF_Q), fwd(OFF_FF), fwd(OFF_I), bwd(OFF_Q), bwd(OFF_FB), bwd(OFF_I),
                     pl.BlockSpec((1, w), lambda h, c, u=u: (0, nb * h + u))]
        args += [proj] * 6 + [lb]
        out_specs += [pl.BlockSpec((C, w), lambda h, c: (c, h)),
                      pl.BlockSpec((C, w), lambda h, c: (nc - 1 - c, h))]
    n_chain = 2 * nb * HG_PAIR
    outs = pl.pallas_call(
        _hg_kernel,
        grid=(HG_HEADS // (HG_PAIR * nb), nc),
        in_specs=in_specs + [const(a) for a in consts],
        out_specs=out_specs,
        out_shape=[jax.ShapeDtypeStruct((s, D_MODEL // nb), bf16)] * (2 * nb),
        scratch_shapes=[pltpu.VMEM((n_chain, HG_HEAD_DIM, HG_HEAD_DIM), f32),
                        pltpu.VMEM((n_chain, C, LANE), f32)],
        compiler_params=_params(("parallel", "arbitrary")),
        name="hgrn",
    )(*args, *consts)
    return outs


HG_GATE_COLS = 256


def _hg_out_kernel(*refs):
    nb = HG_BLOCKS
    scan_refs, w_ref = refs[:2 * nb], refs[2 * nb]
    g_refs, y_ref = refs[2 * nb + 1:-1], refs[-1]
    w = HG_PAIR * LANE
    parts = []
    for u in range(nb):
        parts.append(scan_refs[2 * u][...].astype(f32) + scan_refs[2 * u + 1][...].astype(f32))
    cols = [parts[blk % nb][:, (blk // nb) * w:(blk // nb + 1) * w] for blk in range(D_MODEL // w)]
    o = jnp.concatenate(cols, axis=1)
    g = jnp.concatenate([r[...] for r in g_refs], axis=1).astype(f32)
    rs = lax.rsqrt(jnp.mean(o * o, axis=-1, keepdims=True) + RMS_EPS)
    y_ref[...] = (o * rs * w_ref[...] * _silu(g)).astype(y_ref.dtype)


def _hg_out(scans, proj, norm_w):
    s = proj.shape[0]
    d = D_MODEL
    tm = 256
    cw = HG_GATE_COLS
    tile = pl.BlockSpec((tm, d), lambda i: (i, 0))
    part = pl.BlockSpec((tm, d // HG_BLOCKS), lambda i: (i, 0))
    gates = [pl.BlockSpec((tm, cw), functools.partial(lambda i, u: (i, OFF_G // cw + u), u=u))
             for u in range(d // cw)]
    return pl.pallas_call(
        _hg_out_kernel,
        grid=(s // tm,),
        in_specs=[part] * len(scans) + [pl.BlockSpec((1, d), lambda i: (0, 0))] + gates,
        out_specs=tile,
        out_shape=jax.ShapeDtypeStruct((s, d), bf16),
        compiler_params=_params(("parallel",)),
        name="hgrn_out",
    )(*scans, norm_w.reshape(1, d), *([proj] * len(gates)))


def _post_mix_kernel(y_ref, x_ref, gm_ref, wpost_ref, wpre_ref, sc_ref, sh_ref, wr_ref,
                     x1_ref, hp_ref, lg_ref):
    y = y_ref[...]
    yn = y * lax.rsqrt(jnp.mean(y * y, axis=-1, keepdims=True) + RMS_EPS) * wpost_ref[...]
    x1 = x_ref[...] + gm_ref[...] * yn
    x1_ref[...] = x1
    h = x1 * lax.rsqrt(jnp.mean(x1 * x1, axis=-1, keepdims=True) + RMS_EPS) * wpre_ref[...]
    h = h * (1.0 + sc_ref[...]) + sh_ref[...]
    hhi, hlo = _split2(h)
    whi, wlo = _split2(wr_ref[...])
    lg_ref[...] = _dot(hhi, whi) + _dot(hhi, wlo) + _dot(hlo, whi)
    half = h.shape[1] // 2
    top = pltpu.bitcast(hhi[:, :half].astype(f32), jnp.uint32)
    bot = pltpu.bitcast(hhi[:, half:].astype(f32), jnp.uint32)
    hp_ref[...] = top | (bot >> 16)


def _post_mix(y, x, gm, wpost, wpre, sc, sh, wr_pad):
    s, d = x.shape
    tm = 256
    row = pl.BlockSpec((1, d), lambda i: (0, 0))
    tile = pl.BlockSpec((tm, d), lambda i: (i, 0))
    return pl.pallas_call(
        _post_mix_kernel,
        grid=(s // tm,),
        in_specs=[tile, tile, row, row, row, row, row, pl.BlockSpec((d, LANE), lambda i: (0, 0))],
        out_specs=[tile, pl.BlockSpec((tm, d // 2), lambda i: (i, 0)),
                   pl.BlockSpec((tm, LANE), lambda i: (i, 0))],
        out_shape=[jax.ShapeDtypeStruct((s, d), f32), jax.ShapeDtypeStruct((s, d // 2), jnp.uint32),
                   jax.ShapeDtypeStruct((s, LANE), f32)],
        compiler_params=_params(("parallel",), VMEM_LIMIT),
        name="post_mix",
    )(y, x, gm, wpost.reshape(1, d), wpre.reshape(1, d), sc, sh, wr_pad)


def _select_kernel(lg_ref, utri_ref, idx_ref, gate_ref, pos_ref, sel_ref,
                   aff_ref, cin_ref, *, seq, cap):
    E = N_EXPERTS
    nb = seq // LANE
    lg = lg_ref[...]
    lane = lax.broadcasted_iota(jnp.int32, lg.shape, 1)
    lg = jnp.where(lane < E, lg, -jnp.inf)
    mx = jnp.max(lg, axis=-1, keepdims=True)
    ex = jnp.exp(lg - mx)
    aff = ex / jnp.sum(ex, axis=-1, keepdims=True)
    aff_t = aff.T[:E, :]
    aff_ref[...] = aff_t
    bits = pltpu.bitcast(aff_t, jnp.int32)

    def search(i, thr):
        cand = thr | (jnp.int32(1) << (30 - i))
        cnt = jnp.sum((bits >= cand).astype(jnp.int32), axis=-1, keepdims=True)
        return jnp.where(cnt >= cap, cand, thr)

    thr = lax.fori_loop(0, 31, search, jnp.zeros((E, 1), jnp.int32))
    gt = bits > thr
    eq = bits == thr
    need = cap - jnp.sum(gt.astype(jnp.int32), axis=-1, keepdims=True)
    utri = utri_ref[...]

    def cumsum_excl(mask_f):
        carry = jnp.zeros((E, 1), f32)
        outs = []
        for b in range(nb):
            blk = mask_f[:, b * LANE:(b + 1) * LANE]
            inc = _dot(blk.astype(bf16), utri)
            outs.append(inc - blk + carry)
            carry = carry + inc[:, LANE - 1:LANE]
        return jnp.concatenate(outs, axis=1)

    eq_f = eq.astype(f32)
    rank_eq = cumsum_excl(eq_f)
    sel = gt | (eq & (rank_eq < need.astype(f32)))
    sel_f = sel.astype(f32)
    pos = cumsum_excl(sel_f)
    pos_ref[...] = pos
    sel_ref[...] = sel_f
    cin_ref[...] = jnp.where(sel, pos + 1.0, 0.0)
    pcol = lax.broadcasted_iota(jnp.int32, (cap, LANE), 0).astype(f32)
    lanes = lax.broadcasted_iota(jnp.int32, (cap, LANE), 1)
    idx_m = jnp.zeros((cap, LANE), f32)
    gate_m = jnp.zeros((cap, LANE), f32)
    for e in range(E):
        def blk_body(b, carry):
            cnt_acc, g_acc = carry
            c0 = pl.multiple_of(b * LANE, LANE)
            posr = pos_ref[pl.ds(e, 1), pl.ds(c0, LANE)]
            selr = sel_ref[pl.ds(e, 1), pl.ds(c0, LANE)]
            cinr = cin_ref[pl.ds(e, 1), pl.ds(c0, LANE)]
            affr = aff_ref[pl.ds(e, 1), pl.ds(c0, LANE)]
            incl = posr + selr
            cnt_acc = cnt_acc + jnp.where(incl <= pcol, 1.0, 0.0)
            g_acc = g_acc + jnp.where(cinr == pcol + 1.0, affr, 0.0)
            return cnt_acc, g_acc

        zero = jnp.zeros((cap, LANE), f32)
        cnt_acc, g_acc = lax.fori_loop(0, nb, blk_body, (zero, zero))
        idx_col = jnp.sum(cnt_acc, axis=-1, keepdims=True)
        gate_col = jnp.sum(g_acc, axis=-1, keepdims=True)
        idx_m = jnp.where(lanes == e, idx_col, idx_m)
        gate_m = jnp.where(lanes == e, gate_col, gate_m)
    idx_ref[...] = idx_m.astype(jnp.int32)
    gate_ref[...] = gate_m


def _select(logits, cap):
    s = logits.shape[0]
    t = np.arange(LANE)
    utri = jnp.asarray((t[:, None] <= t[None, :]).astype(np.float32), bf16)
    full = lambda shape: pl.BlockSpec(shape, lambda: tuple(0 for _ in shape))
    return pl.pallas_call(
        functools.partial(_select_kernel, seq=s, cap=cap),
        in_specs=[full((s, LANE)), full((LANE, LANE))],
        out_specs=[full((cap, LANE)), full((cap, LANE)), full((N_EXPERTS, s)), full((N_EXPERTS, s))],
        out_shape=[jax.ShapeDtypeStruct((cap, LANE), jnp.int32), jax.ShapeDtypeStruct((cap, LANE), f32),
                   jax.ShapeDtypeStruct((N_EXPERTS, s), f32), jax.ShapeDtypeStruct((N_EXPERTS, s), f32)],
        scratch_shapes=[pltpu.VMEM((N_EXPERTS, s), f32), pltpu.VMEM((N_EXPERTS, s), f32)],
        compiler_params=_params(None, VMEM_LIMIT),
        name="ec_select",
    )(logits, utri)


FF_TILE = 256
DOWN_TILE = 512


def _expert_kernel(idx_ref, hp_ref, wg_ref, wu_ref, wd_ref, gate_ref, y_ref,
                   xbuf_ref, xa_ref, xb_ref, hid_ref, sem, *, cap):
    e = pl.program_id(0)
    s = pl.program_id(1)
    n_ff = D_FF_EXPERT // FF_TILE
    half = D_MODEL // 2

    def gather(ex):
        def start(p, c):
            pltpu.make_async_copy(hp_ref.at[pl.ds(idx_ref[ex, p], 1), :],
                                  xbuf_ref.at[pl.ds(p, 1), :], sem).start()
            return c

        lax.fori_loop(0, cap, start, 0)

    @pl.when((s == 0) & (e == 0))
    def _():
        gather(0)

    @pl.when(s == 0)
    def _():
        pltpu.make_async_copy(hp_ref.at[pl.ds(0, cap), :], xbuf_ref, sem).wait()
        u = xbuf_ref[...]
        xa_ref[...] = pltpu.bitcast(u & jnp.uint32(0xFFFF0000), f32).astype(bf16)
        xb_ref[...] = pltpu.bitcast(u << 16, f32).astype(bf16)

    @pl.when((s == 0) & (e + 1 < N_EXPERTS))
    def _():
        gather(e + 1)

    @pl.when(s < n_ff)
    def _():
        wg = wg_ref[0].astype(bf16)
        wu = wu_ref[0].astype(bf16)
        xa = xa_ref[...]
        xb = xb_ref[...]
        hg = _dot(xa, wg[:half]) + _dot(xb, wg[half:])
        hu = _dot(xa, wu[:half]) + _dot(xb, wu[half:])
        c0 = pl.multiple_of(s * FF_TILE, FF_TILE)
        hid_ref[:, pl.ds(c0, FF_TILE)] = (_silu(hg) * hu).astype(bf16)

    @pl.when(s >= n_ff)
    def _():
        y = _dot(hid_ref[...], wd_ref[0].astype(bf16))
        y_ref[0] = (y * gate_ref[0]).astype(y_ref.dtype)


def _experts(idx, hp, w_gate, w_up, w_down, gate, cap):
    n_ff = D_FF_EXPERT // FF_TILE
    n_dn = D_MODEL // DOWN_TILE
    ffj = lambda s: jnp.minimum(s, n_ff - 1)
    dnj = lambda s: jnp.maximum(s - n_ff, 0)
    grid_spec = pltpu.PrefetchScalarGridSpec(
        num_scalar_prefetch=1,
        grid=(N_EXPERTS, n_ff + n_dn),
        in_specs=[pl.BlockSpec(memory_space=pl.ANY),
                  pl.BlockSpec((1, D_MODEL, FF_TILE), lambda e, s, idx: (e, 0, ffj(s))),
                  pl.BlockSpec((1, D_MODEL, FF_TILE), lambda e, s, idx: (e, 0, ffj(s))),
                  pl.BlockSpec((1, D_FF_EXPERT, DOWN_TILE), lambda e, s, idx: (e, 0, dnj(s))),
                  pl.BlockSpec((1, cap, 1), lambda e, s, idx: (e, 0, 0))],
        out_specs=pl.BlockSpec((1, cap, DOWN_TILE), lambda e, s, idx: (e, 0, dnj(s))),
        scratch_shapes=[pltpu.VMEM((cap, D_MODEL // 2), jnp.uint32),
                        pltpu.VMEM((cap, D_MODEL // 2), bf16),
                        pltpu.VMEM((cap, D_MODEL // 2), bf16),
                        pltpu.VMEM((cap, D_FF_EXPERT), bf16),
                        pltpu.SemaphoreType.DMA(())],
    )
    return pl.pallas_call(
        functools.partial(_expert_kernel, cap=cap),
        grid_spec=grid_spec,
        out_shape=jax.ShapeDtypeStruct((N_EXPERTS, cap, D_MODEL), bf16),
        compiler_params=_params(("arbitrary", "arbitrary"), VMEM_LIMIT),
        name="ec_experts",
    )(idx, hp, w_gate, w_up, w_down, gate)


COMBINE_ALIGN = 16


COMBINE_K = 256


def _combine_plan(pos, cap, tt):
    al = COMBINE_ALIGN
    first = pos[:, ::tt].astype(jnp.int32)
    nxt = jnp.concatenate([first[:, 1:], jnp.full((first.shape[0], 1), cap, jnp.int32)], axis=1)
    start = first // al * al
    pieces = jnp.where(nxt > first, (nxt - start + al - 1) // al, 0)
    base = (jnp.cumsum(pieces, axis=0) - pieces) * al
    total = jnp.sum(pieces, axis=0)
    chunks = (total * al + COMBINE_K - 1) // COMBINE_K
    return start, pieces, base, total, chunks


def _combine_kernel(start_ref, pieces_ref, base_ref, total_ref, chunks_ref,
                    y_ref, pos_ref, sel_ref, x1_ref, gf_ref, w_ref, o_ref,
                    stage_ref, acc_ref, sem, *, tt):
    i = pl.program_id(0)
    al = COMBINE_ALIGN

    @pl.when(i == 0)
    def _():
        stage_ref[...] = jnp.zeros_like(stage_ref)

    for e in range(N_EXPERTS):
        def issue(j, c):
            src = pl.multiple_of(start_ref[e, i] + j * al, al)
            dst = pl.multiple_of(base_ref[e, i] + j * al, al)
            pltpu.make_async_copy(y_ref.at[e, pl.ds(src, al), :], stage_ref.at[pl.ds(dst, al), :], sem).start()
            return c

        lax.fori_loop(0, pieces_ref[e, i], issue, 0)

    keys = []
    for e in range(N_EXPERTS):
        shift = (base_ref[e, i] - start_ref[e, i]).astype(f32)
        keys.append(jnp.where(sel_ref[e:e + 1, :] > 0.0, pos_ref[e:e + 1, :] + shift, -1.0))

    def wait(j, c):
        pltpu.make_async_copy(y_ref.at[0, pl.ds(0, al), :], stage_ref.at[pl.ds(0, al), :], sem).wait()
        return c

    lax.fori_loop(0, total_ref[i], wait, 0)
    acc_ref[...] = jnp.zeros_like(acc_ref)
    riota = lax.broadcasted_iota(jnp.int32, (COMBINE_K, tt), 0).astype(f32)

    def chunk(k, c):
        r0 = pl.multiple_of(k * COMBINE_K, COMBINE_K)
        rows = riota + r0.astype(f32)
        et = jnp.zeros((COMBINE_K, tt), f32)
        for e in range(N_EXPERTS):
            et = jnp.where(keys[e] == rows, 1.0, et)
        acc_ref[...] += lax.dot_general(et.astype(bf16), stage_ref[pl.ds(r0, COMBINE_K), :],
                                        (((0,), (0,)), ((), ())), preferred_element_type=f32)
        return c

    lax.fori_loop(0, chunks_ref[i], chunk, 0)
    y = acc_ref[...]
    yn = y * lax.rsqrt(jnp.mean(y * y, axis=-1, keepdims=True) + RMS_EPS) * w_ref[...]
    o_ref[...] = x1_ref[...] + gf_ref[...] * yn


def _combine(plan, yexp, pos, sel, x1, gf, w_post, tt):
    s, d = x1.shape
    stage_rows = N_EXPERTS * (tt + 2 * COMBINE_ALIGN)
    stage_rows = (stage_rows + COMBINE_K - 1) // COMBINE_K * COMBINE_K
    row = pl.BlockSpec((1, d), lambda i, *_: (0, 0))
    grid_spec = pltpu.PrefetchScalarGridSpec(
        num_scalar_prefetch=5,
        grid=(s // tt,),
        in_specs=[pl.BlockSpec(memory_space=pl.ANY),
                  pl.BlockSpec((N_EXPERTS, tt), lambda i, *_: (0, i)),
                  pl.BlockSpec((N_EXPERTS, tt), lambda i, *_: (0, i)),
                  pl.BlockSpec((tt, d), lambda i, *_: (i, 0)), row, row],
        out_specs=pl.BlockSpec((tt, d), lambda i, *_: (i, 0)),
        scratch_shapes=[pltpu.VMEM((stage_rows, d), bf16), pltpu.VMEM((tt, d), f32),
                        pltpu.SemaphoreType.DMA(())],
    )
    return pl.pallas_call(
        functools.partial(_combine_kernel, tt=tt),
        grid_spec=grid_spec,
        out_shape=jax.ShapeDtypeStruct((s, d), f32),
        compiler_params=_params(("arbitrary",), VMEM_LIMIT),
        name="ec_combine",
    )(*plan, yexp, pos, sel, x1, gf, w_post.reshape(1, d))


GATE_SUB = 256


def _gate_extras(proj, off, tm, tn):
    n_sub = tn // GATE_SUB
    return [(proj, (tm, GATE_SUB), functools.partial(
        lambda m, j, u: (m, off // GATE_SUB + j * n_sub + u), u=u)) for u in range(n_sub)]


def _merge_a(acc, *gate_refs):
    ga = jnp.concatenate([r[...] for r in gate_refs], axis=1).astype(f32)
    return _sigmoid(ga) * acc


def _merge_b(acc, part_ref, *gate_refs):
    gb = jnp.concatenate([r[...] for r in gate_refs], axis=1).astype(f32)
    return part_ref[...] + _sigmoid(gb) * acc


def _layer(x, mod, lb, norm_pre_mix, norm_post_mix, norm_pre_ffn, norm_post_ffn,
           w_in, conv_w, conv_b, dt_bias, a_log, d_skip, ssd_norm_w, hg_norm_w,
           w_ssd_out, w_hg_out, w_mix_out, w_router, w_gate, w_up, w_down):
    s, d = x.shape
    sh_m, sc_m, g_m, sh_f, sc_f, g_f = [mod[:, i * d:(i + 1) * d] for i in range(6)]

    h = _prenorm(x, norm_pre_mix, sc_m, sh_m)
    tm8, tm16 = s // 8, s // 16
    proj = _wsmm(h, w_in, tm=tm16, tn=1280, out_dtype=bf16, name="in_proj")

    xact = _conv(proj, conv_w, conv_b)
    tri_f, tri_b, sel_a, sel_w, ea, ew = _ssd_consts()
    cb = lambda a: jnp.asarray(a, bf16)
    paf, pwf, arf, pab, pwb, arb = _ssd_prep(proj, dt_bias, a_log,
                                             (cb(tri_f), cb(tri_b), cb(sel_a), cb(sel_w)))
    y_f = _ssd(xact, paf, pwf, arf, cb(ea), cb(ew), rev=False)
    dskip = jnp.repeat(d_skip, SSD_HEAD_DIM).reshape(1, SSD_D_INNER)
    y_ssd = _ssd(xact, pab, pwb, arb, cb(ea), cb(ew), rev=True,
                 final_args=(y_f, proj, ssd_norm_w.reshape(1, SSD_D_INNER), dskip))

    hg_consts = [tuple((cb(tri), cb(tsm), jnp.asarray(msk))) for tri, tsm, msk in
                 (_hg_consts(False), _hg_consts(True))]
    scans = _hgrn(proj, lb, hg_consts[0], hg_consts[1])
    y_hg = _hg_out(scans, proj, hg_norm_w)

    part = _wsmm(y_ssd, w_ssd_out, tm=tm16, tn=512, out_dtype=f32,
                 extras=_gate_extras(proj, OFF_GA, tm16, 512),
                 epilogue=_merge_a, name="ssd_out_proj")
    merged = _wsmm(y_hg, w_hg_out, tm=tm8, tn=512, out_dtype=bf16,
                   extras=[(part, (tm8, 512), lambda m, j: (m, j))]
                   + _gate_extras(proj, OFF_GB, tm8, 512),
                   epilogue=_merge_b, name="hg_out_proj")
    y_mix = _wsmm(merged, w_mix_out, tm=tm8, tn=512, out_dtype=f32, name="mix_out_proj")

    wr_pad = jnp.pad(w_router, ((0, 0), (0, LANE - N_EXPERTS)))
    x1, hp, logits = _post_mix(y_mix, x, g_m, norm_post_mix, norm_pre_ffn, sc_f, sh_f, wr_pad)
    cap = 2 * s // N_EXPERTS
    idx_m, gate_m, pos, sel = _select(logits, cap)
    idx = idx_m[:, :N_EXPERTS].T
    gate = gate_m[:, :N_EXPERTS].T.reshape(N_EXPERTS, cap, 1)
    yexp = _experts(idx, hp, w_gate, w_up, w_down, gate, cap)
    tt = min(128, cap // 2)
    plan = _combine_plan(pos, cap, tt)
    return _combine(plan, yexp, pos, sel, x1, g_f, norm_post_ffn, tt)


def kernel(x, c, w_ada, b_ada, norm_pre_mix, norm_post_mix, norm_pre_ffn, norm_post_ffn, w_in, conv_w, conv_b, dt_bias_fwd, dt_bias_bwd, a_log_fwd, a_log_bwd, d_skip, ssd_norm_w, hg_lower_bound, hg_norm_w, w_ssd_out, w_hg_out, w_mix_out, w_router, w_gate, w_up, w_down):
    depth = w_ada.shape[0]
    lower_bounds = jnp.cumsum(jax.nn.softmax(hg_lower_bound.astype(f32), axis=0), axis=0)
    outs = []
    for bi in range(x.shape[0]):
        xb = x[bi]
        cb_ = c[bi:bi + 1]
        for l in range(depth):
            mod = _ada(cb_, w_ada[l], b_ada[l])
            dt_bias = jnp.concatenate([dt_bias_fwd[l], dt_bias_bwd[l]]).reshape(1, -1)
            a_log = jnp.concatenate([a_log_fwd[l], a_log_bwd[l]]).reshape(1, -1)
            xb = _layer(xb, mod, lower_bounds[l].reshape(1, -1), norm_pre_mix[l], norm_post_mix[l],
                        norm_pre_ffn[l], norm_post_ffn[l], w_in[l], conv_w[l], conv_b[l],
                        dt_bias, a_log, d_skip[l], ssd_norm_w[l], hg_norm_w[l],
                        w_ssd_out[l], w_hg_out[l], w_mix_out[l], w_router[l],
                        w_gate[l], w_up[l], w_down[l])
        outs.append(xb)
    return jnp.stack(outs, axis=0)
```

```python
import functools

import numpy as np
import jax
import jax.numpy as jnp
from jax import lax
from jax.experimental import pallas as pl
from jax.experimental.pallas import tpu as pltpu

f32 = jnp.float32
bf16 = jnp.bfloat16

D_MODEL = 4096
SSD_D_INNER = 8192
SSD_HEADS = 128
SSD_HEAD_DIM = 64
SSD_D_STATE = 128
SSD_GROUPS = 8
SSD_HEADS_PER_GROUP = 16
SSD_CONV = 5
SSD_XBC = SSD_D_INNER + 2 * SSD_GROUPS * SSD_D_STATE
HG_HEADS = 32
HG_HEAD_DIM = 128
N_EXPERTS = 16
D_FF_EXPERT = 2048
RMS_EPS = 1e-6

OFF_Z = 0
OFF_XBC = OFF_Z + SSD_D_INNER
OFF_DT = OFF_XBC + SSD_XBC
OFF_Q = OFF_DT + 2 * SSD_HEADS
OFF_FF = OFF_Q + D_MODEL
OFF_FB = OFF_FF + D_MODEL
OFF_I = OFF_FB + D_MODEL
OFF_G = OFF_I + D_MODEL
OFF_GA = OFF_G + D_MODEL
OFF_GB = OFF_GA + D_MODEL
D_IN_PROJ = OFF_GB + D_MODEL

LANE = 128
CHUNK = 128
VMEM_LIMIT = 56 * 1024 * 1024


def _params(sem, vmem=None):
    return pltpu.CompilerParams(dimension_semantics=sem, vmem_limit_bytes=vmem)


def _sigmoid(x):
    return 1.0 / (1.0 + jnp.exp(-x))


def _silu(x):
    return x * _sigmoid(x)


def _softplus(x):
    return jnp.maximum(x, 0.0) + jnp.log(1.0 + jnp.exp(-jnp.abs(x)))


def _split2(x):
    hi = x.astype(bf16)
    lo = (x - hi.astype(f32)).astype(bf16)
    return hi, lo


def _split3(x):
    hi = x.astype(bf16)
    r = x - hi.astype(f32)
    mid = r.astype(bf16)
    lo = (r - mid.astype(f32)).astype(bf16)
    return hi, mid, lo


def _dot(a, b):
    return jnp.dot(a, b, preferred_element_type=f32)


def _dot_nt(a, b):
    return lax.dot_general(a, b, (((1,), (1,)), ((), ())), preferred_element_type=f32)


def _ada_kernel(c_ref, w_ref, b_ref, o_ref):
    c = c_ref[...]
    ca = jnp.broadcast_to(_silu(c), (8, c.shape[1]))
    chi, clo = _split2(ca)
    whi, wlo = _split2(w_ref[...])
    acc = _dot(chi, whi) + _dot(chi, wlo) + _dot(clo, whi)
    o_ref[...] = acc[0:1, :] + b_ref[...]


def _ada(c, w, b):
    d, n = w.shape
    tn = 512
    return pl.pallas_call(
        _ada_kernel,
        grid=(n // tn,),
        in_specs=[pl.BlockSpec((1, d), lambda j: (0, 0)),
                  pl.BlockSpec((d, tn), lambda j: (0, j)),
                  pl.BlockSpec((1, tn), lambda j: (0, j))],
        out_specs=pl.BlockSpec((1, tn), lambda j: (0, j)),
        out_shape=jax.ShapeDtypeStruct((1, n), f32),
        compiler_params=_params(("parallel",), VMEM_LIMIT),
        name="adaln",
    )(c, w, b.reshape(1, n))


def _prenorm_kernel(x_ref, w_ref, sc_ref, sh_ref, o_ref):
    x = x_ref[...]
    r = lax.rsqrt(jnp.mean(x * x, axis=-1, keepdims=True) + RMS_EPS)
    h = (x * r * w_ref[...]) * (1.0 + sc_ref[...]) + sh_ref[...]
    o_ref[...] = h.astype(o_ref.dtype)


def _prenorm(x, w, sc, sh):
    s, d = x.shape
    tm = 256
    row = pl.BlockSpec((1, d), lambda i: (0, 0))
    return pl.pallas_call(
        _prenorm_kernel,
        grid=(s // tm,),
        in_specs=[pl.BlockSpec((tm, d), lambda i: (i, 0)), row, row, row],
        out_specs=pl.BlockSpec((tm, d), lambda i: (i, 0)),
        out_shape=jax.ShapeDtypeStruct((s, d), bf16),
        compiler_params=_params(("parallel",)),
        name="prenorm",
    )(x, w.reshape(1, d), sc, sh)


def _wsmm_kernel(*refs, n_extra, nn, epilogue):
    a_ref, w_ref = refs[0], refs[1]
    extras = refs[2:2 + n_extra]
    o_ref = refs[2 + n_extra]
    wbuf_ref = refs[3 + n_extra]
    n = pl.program_id(0)
    m = pl.program_id(1)
    kc = w_ref.shape[0]

    @pl.when(n < nn)
    def _():
        r0 = pl.multiple_of(m * kc, kc)
        wbuf_ref[n % 2, pl.ds(r0, kc), :] = w_ref[...].astype(bf16)

    @pl.when(n == 0)
    def _():
        o_ref[...] = jnp.zeros_like(o_ref)

    @pl.when(n > 0)
    def _():
        acc = _dot(a_ref[...], wbuf_ref[(n - 1) % 2])
        o_ref[...] = epilogue(acc, *extras).astype(o_ref.dtype)


def _wsmm(a, w, *, tm, tn, out_dtype, extras=(), epilogue=None, name="matmul"):
    m, kdim = a.shape
    n = w.shape[1]
    nm, nn = m // tm, n // tn
    kc = kdim // nm
    if epilogue is None:
        epilogue = lambda acc: acc
    jn = lambda nidx: jnp.maximum(nidx - 1, 0)
    extra_arrays = [e[0] for e in extras]
    extra_specs = [pl.BlockSpec(e[1], functools.partial(lambda nidx, midx, f: f(midx, jn(nidx)), f=e[2]))
                   for e in extras]
    return pl.pallas_call(
        functools.partial(_wsmm_kernel, n_extra=len(extras), nn=nn, epilogue=epilogue),
        grid=(nn + 1, nm),
        in_specs=[pl.BlockSpec((tm, kdim), lambda nidx, midx: (jnp.where(nidx == 0, 0, midx), 0)),
                  pl.BlockSpec((kc, tn), lambda nidx, midx: (midx, jnp.minimum(nidx, nn - 1)))] + extra_specs,
        out_specs=pl.BlockSpec((tm, tn), lambda nidx, midx: (jnp.where(nidx == 0, 0, midx), jn(nidx))),
        out_shape=jax.ShapeDtypeStruct((m, n), out_dtype),
        scratch_shapes=[pltpu.VMEM((2, kdim, tn), bf16)],
        compiler_params=_params(("arbitrary", "arbitrary"), VMEM_LIMIT),
        name=name,
    )(a, w, *extra_arrays)


CONV_HALO = 16


def _conv_kernel(x_ref, w_ref, b_ref, o_ref, *, seq, tile):
    w = w_ref[...]
    b = b_ref[...]
    nt = seq // tile
    ext_rows = tile + 2 * CONV_HALO

    def body(i, carry):
        r0 = pl.multiple_of(i * tile, tile)
        cur = x_ref[pl.ds(r0, tile), :].astype(f32)
        p0 = pl.multiple_of(jnp.maximum(r0 - CONV_HALO, 0), CONV_HALO)
        n0 = pl.multiple_of(jnp.minimum(r0 + tile, seq - CONV_HALO), CONV_HALO)
        prev = jnp.where(i > 0, x_ref[pl.ds(p0, CONV_HALO), :].astype(f32), 0.0)
        nxt = jnp.where(i < nt - 1, x_ref[pl.ds(n0, CONV_HALO), :].astype(f32), 0.0)
        ext = jnp.concatenate([prev, cur, nxt], axis=0)
        acc = b + w[2:3, :] * cur
        for k in (0, 1, 3, 4):
            d = k - SSD_CONV // 2
            shifted = pltpu.roll(ext, (-d) % ext_rows, axis=0)[CONV_HALO:CONV_HALO + tile, :]
            acc = acc + w[k:k + 1, :] * shifted
        o_ref[pl.ds(r0, tile), :] = _silu(acc).astype(o_ref.dtype)
        return carry

    lax.fori_loop(0, nt, body, 0)


def _conv(proj, conv_w, conv_b):
    s = proj.shape[0]
    cw = 256
    off = OFF_XBC // cw
    tile = min(512, s)
    return pl.pallas_call(
        functools.partial(_conv_kernel, seq=s, tile=tile),
        grid=(SSD_XBC // cw,),
        in_specs=[pl.BlockSpec((s, cw), lambda j: (0, off + j)),
                  pl.BlockSpec((SSD_CONV, cw), lambda j: (0, j)),
                  pl.BlockSpec((1, cw), lambda j: (0, j))],
        out_specs=pl.BlockSpec((s, cw), lambda j: (0, j)),
        out_shape=jax.ShapeDtypeStruct((s, SSD_XBC), bf16),
        compiler_params=_params(("parallel",), VMEM_LIMIT),
        name="conv_silu",
    )(proj, conv_w, conv_b.reshape(1, SSD_XBC))


def _ssd_consts():
    L = CHUNK
    t = np.arange(L)
    tri_f = (t[None, :] <= t[:, None]).astype(np.float32)
    tri_b = (t[None, :] >= t[:, None]).astype(np.float32)
    h = np.arange(SSD_HEADS)
    g, r = h // SSD_HEADS_PER_GROUP, h % SSD_HEADS_PER_GROUP
    sel_a = np.zeros((3 * SSD_HEADS, SSD_GROUPS * LANE), np.float32)
    for j in range(3):
        sel_a[j * SSD_HEADS + h, g * LANE + j * SSD_HEADS_PER_GROUP + r] = 1.0
    sel_w = np.zeros((6 * SSD_HEADS, SSD_GROUPS * LANE), np.float32)
    for j in range(6):
        sel_w[j * SSD_HEADS + h, g * LANE + j * SSD_HEADS_PER_GROUP + r] = 1.0
    ea = np.zeros((LANE, SSD_HEADS_PER_GROUP * L), np.float32)
    for j in range(3):
        for rr in range(SSD_HEADS_PER_GROUP):
            ea[j * SSD_HEADS_PER_GROUP + rr, rr * L:(rr + 1) * L] = 1.0
    gw = SSD_HEADS_PER_GROUP * SSD_HEAD_DIM
    ew = np.zeros((LANE, 3 * gw), np.float32)
    for j in range(6):
        for rr in range(SSD_HEADS_PER_GROUP):
            ew[j * SSD_HEADS_PER_GROUP + rr,
               (j // 2) * gw + rr * SSD_HEAD_DIM:(j // 2) * gw + (rr + 1) * SSD_HEAD_DIM] = 1.0
    return tri_f, tri_b, sel_a, sel_w, ea, ew


def _ssd_prep_kernel(dt_ref, bias_ref, alog_ref, trif_ref, trib_ref, sela_ref, selw_ref,
                     paf_ref, pwf_ref, arf_ref, pab_ref, pwb_ref, arb_ref):
    raw = dt_ref[...].astype(f32)
    L = raw.shape[0]
    outs = ((paf_ref, pwf_ref, arf_ref, trif_ref, L - 1), (pab_ref, pwb_ref, arb_ref, trib_ref, 0))
    for d, (pa_ref, pw_ref, ar_ref, tri_ref, tot_row) in enumerate(outs):
        sl = slice(d * SSD_HEADS, (d + 1) * SSD_HEADS)
        dt = _softplus(raw[:, sl] + bias_ref[:, sl])
        adt = dt * (-jnp.exp(alog_ref[:, sl]))
        h3 = jnp.concatenate(_split3(adt), axis=1)
        a3 = _dot(tri_ref[...], h3)
        a = (a3[:, :SSD_HEADS] + a3[:, SSD_HEADS:2 * SSD_HEADS]) + a3[:, 2 * SSD_HEADS:]
        atot = a[tot_row:tot_row + 1, :]
        wgt = jnp.exp(atot - a) * dt
        ea = jnp.exp(a)
        ar_ref[...] = a.T
        pa_ref[...] = _dot(jnp.concatenate(_split3(a), axis=1), sela_ref[...]).astype(bf16)
        cols = _split2(wgt) + _split2(ea) + _split2(dt)
        pw_ref[...] = _dot(jnp.concatenate(cols, axis=1), selw_ref[...]).astype(bf16)


def _ssd_prep(proj, dt_bias, a_log, consts):
    s = proj.shape[0]
    L = CHUNK
    tri_f, tri_b, sel_a, sel_w = consts
    gl = SSD_GROUPS * LANE
    const = lambda shape: pl.BlockSpec(shape, lambda c: (0, 0))
    packed = pl.BlockSpec((L, gl), lambda c: (c, 0))
    rowf = pl.BlockSpec((SSD_HEADS, L), lambda c: (0, c))
    return pl.pallas_call(
        _ssd_prep_kernel,
        grid=(s // L,),
        in_specs=[pl.BlockSpec((L, 2 * SSD_HEADS), lambda c: (c, OFF_DT // (2 * SSD_HEADS))),
                  const((1, 2 * SSD_HEADS)), const((1, 2 * SSD_HEADS)),
                  const((L, L)), const((L, L)),
                  const(sel_a.shape), const(sel_w.shape)],
        out_specs=[packed, packed, rowf, packed, packed, rowf],
        out_shape=[jax.ShapeDtypeStruct((s, gl), bf16), jax.ShapeDtypeStruct((s, gl), bf16),
                   jax.ShapeDtypeStruct((SSD_HEADS, s), f32)] * 2,
        compiler_params=_params(("parallel",)),
        name="ssd_prep",
    )(proj, dt_bias, a_log, tri_f, tri_b, sel_a, sel_w)


def _ssd_kernel(*refs, rev, final):
    if final:
        (x_ref, b_ref, c_ref, pa_ref, pw_ref, ar_ref, ea_ref, ew_ref,
         yprev_ref, z_ref, nw_ref, dsk_ref, o_ref, st_ref) = refs
    else:
        x_ref, b_ref, c_ref, pa_ref, pw_ref, ar_ref, ea_ref, ew_ref, o_ref, st_ref = refs
    L = CHUNK
    gw = SSD_HEADS_PER_GROUP * SSD_HEAD_DIM
    npair = SSD_HEADS_PER_GROUP // 2

    @pl.when(pl.program_id(1) == 0)
    def _():
        st_ref[...] = jnp.zeros_like(st_ref)

    x = x_ref[...]
    bm = b_ref[...]
    cm = c_ref[...]
    aexp = _dot(pa_ref[...], ea_ref[...])
    wed = _dot(pw_ref[...], ew_ref[...])
    wexp, eexp, dtexp = wed[:, :gw], wed[:, gw:2 * gw], wed[:, 2 * gw:]
    xf = x.astype(f32)
    xdt = (xf * dtexp).astype(bf16)
    xw = (xf * wexp).astype(bf16)
    tot_row = 0 if rev else L - 1
    dec_tot = eexp[tot_row:tot_row + 1, :]
    bt = bm.astype(f32).T.astype(bf16)
    cb = _dot(cm, bt)
    ti = lax.broadcasted_iota(jnp.int32, (L, L), 0)
    si = lax.broadcasted_iota(jnp.int32, (L, L), 1)
    keep = (si >= ti) if rev else (si <= ti)
    cbm = jnp.where(keep, cb, 0.0)
    lane = lax.broadcasted_iota(jnp.int32, (L, LANE), 1)
    first = lane < SSD_HEAD_DIM
    ys = []
    for j in range(npair):
        ds_ = []
        for r in (2 * j, 2 * j + 1):
            diff = aexp[:, r * L:(r + 1) * L] - ar_ref[r:r + 1, :]
            ds_.append((jnp.exp(jnp.minimum(diff, 0.0)) * cbm).astype(bf16))
        mp = jnp.concatenate(ds_, axis=1)
        xp = xdt[:, j * LANE:(j + 1) * LANE]
        zero = jnp.zeros_like(xp)
        xbd = jnp.concatenate([jnp.where(first, xp, zero), jnp.where(first, zero, xp)], axis=0)
        st = st_ref[j]
        y = _dot(mp, xbd) + _dot(cm, st.astype(bf16)) * eexp[:, j * LANE:(j + 1) * LANE]
        ys.append(y)
        st_ref[j] = st * dec_tot[:, j * LANE:(j + 1) * LANE] + _dot(bt, xw[:, j * LANE:(j + 1) * LANE])
    y = jnp.concatenate(ys, axis=1)
    if final:
        y = y + yprev_ref[...].astype(f32) + dsk_ref[...] * xf
        y = y * _silu(z_ref[...].astype(f32))
        y = y * lax.rsqrt(jnp.mean(y * y, axis=-1, keepdims=True) + RMS_EPS) * nw_ref[...]
    o_ref[...] = y.astype(o_ref.dtype)


def _ssd(xact, pa, pw, arow, ea, ew, *, rev, final_args=None):
    s = xact.shape[0]
    L = CHUNK
    nc = s // L
    gw = SSD_HEADS_PER_GROUP * SSD_HEAD_DIM
    cc = (lambda c: nc - 1 - c) if rev else (lambda c: c)
    nxg = SSD_D_INNER // LANE
    in_specs = [pl.BlockSpec((L, gw), lambda g, c: (cc(c), g)),
                pl.BlockSpec((L, LANE), lambda g, c: (cc(c), nxg + g)),
                pl.BlockSpec((L, LANE), lambda g, c: (cc(c), nxg + SSD_GROUPS + g)),
                pl.BlockSpec((L, LANE), lambda g, c: (cc(c), g)),
                pl.BlockSpec((L, LANE), lambda g, c: (cc(c), g)),
                pl.BlockSpec((SSD_HEADS_PER_GROUP, L), lambda g, c: (g, cc(c))),
                pl.BlockSpec(ea.shape, lambda g, c: (0, 0)),
                pl.BlockSpec(ew.shape, lambda g, c: (0, 0))]
    args = [xact, xact, xact, pa, pw, arow, ea, ew]
    final = final_args is not None
    if final:
        yprev, proj, norm_w, dskip = final_args
        in_specs += [pl.BlockSpec((L, gw), lambda g, c: (cc(c), g)),
                     pl.BlockSpec((L, gw), lambda g, c: (cc(c), OFF_Z // gw + g)),
                     pl.BlockSpec((1, gw), lambda g, c: (0, g)),
                     pl.BlockSpec((1, gw), lambda g, c: (0, g))]
        args += [yprev, proj, norm_w, dskip]
    return pl.pallas_call(
        functools.partial(_ssd_kernel, rev=rev, final=final),
        grid=(SSD_GROUPS, nc),
        in_specs=in_specs,
        out_specs=pl.BlockSpec((L, gw), lambda g, c: (cc(c), g)),
        out_shape=jax.ShapeDtypeStruct((s, SSD_D_INNER), bf16),
        scratch_shapes=[pltpu.VMEM((SSD_HEADS_PER_GROUP // 2, SSD_D_STATE, LANE), f32)],
        compiler_params=_params(("parallel", "arbitrary")),
        name="ssd_bwd" if rev else "ssd_fwd",
    )(*args)


HG_LEVELS = 8
HG_MM_LEVELS = 2
HG_PAIR = 2
HG_BLOCKS = 2
HG_SUB = 2


def _hg_level_ref_row(lv, blk, rev):
    b = 1 << (lv - 1)
    return blk * 2 * b + (b if rev else b - 1)


def _hg_consts(rev):
    C = CHUNK
    t = np.arange(C)
    if rev:
        tri = (t[None, :] >= t[:, None]).astype(np.float32)
    else:
        tri = (t[None, :] <= t[:, None]).astype(np.float32)
    tmats = []
    masks = [(t[:, None] == t[None, :]).astype(np.float32)]
    for lv in range(1, HG_LEVELS):
        b = 1 << (lv - 1)
        blk = t // (2 * b)
        upper = (t % (2 * b)) >= b
        if rev:
            mask = (blk[:, None] == blk[None, :]) & (~upper)[:, None] & upper[None, :]
        else:
            mask = (blk[:, None] == blk[None, :]) & upper[:, None] & (~upper)[None, :]
        if lv <= HG_MM_LEVELS:
            tmats.append(tri - tri[_hg_level_ref_row(lv, blk, rev)])
        masks.append(mask.astype(np.float32))
    return tri, np.concatenate(tmats, axis=0), np.stack(masks, axis=0)


def _hg_chain(qb, fb, vb, lb, tri_ref, tsm_ref, m_ref, g_ref, st_ref, rev):
    C = CHUNK
    qf = qb.astype(f32)
    fr = fb.astype(f32)
    sg = _sigmoid(fr)
    lf = jnp.log2(lb + (1.0 - lb) * sg)
    kk = (1.0 - lb) * (1.0 - sg)
    kb = kk.astype(bf16)
    h3 = _split3(lf)
    g3 = _dot(tri_ref[...], jnp.concatenate(h3, axis=1))
    g = (g3[:, :LANE] + g3[:, LANE:2 * LANE]) + g3[:, 2 * LANE:]
    e2 = _dot(tsm_ref[...], jnp.concatenate(h3[:2], axis=1))
    e_small = e2[:, :LANE] + e2[:, LANE:]
    g_ref[...] = g
    sc = _dot_nt(qb, kb) * m_ref[0]
    for lv in range(1, HG_LEVELS):
        if lv <= HG_MM_LEVELS:
            e = e_small[(lv - 1) * C:lv * C, :]
        else:
            b = 1 << (lv - 1)
            rows = []
            for blk in range(C // (2 * b)):
                m = _hg_level_ref_row(lv, blk, rev)
                rows.append(jnp.broadcast_to(g_ref[m:m + 1, :], (2 * b, LANE)))
            e = g - jnp.concatenate(rows, axis=0)
        xd = jnp.exp2(-jnp.abs(e)).astype(bf16)
        sc = sc + _dot_nt(qb * xd, kb * xd) * m_ref[lv]
    o = _dot(sc.astype(bf16), vb)
    st = st_ref[...]
    o = o + _dot_nt((qf * jnp.exp2(g)).astype(bf16), st.astype(bf16))
    tot_row = 0 if rev else C - 1
    gtot = g[tot_row:tot_row + 1, :]
    kdec = (kk * jnp.exp2(gtot - g)).astype(bf16)
    vt = vb.astype(f32).T.astype(bf16)
    st_ref[...] = st * jnp.exp2(gtot) + _dot(vt, kdec)
    return o


def _hg_kernel(*refs):
    n_in = 7 * HG_BLOCKS
    data = refs[:n_in]
    trif_ref, tsmf_ref, mf_ref, trib_ref, tsmb_ref, mb_ref = refs[n_in:n_in + 6]
    outs = refs[n_in + 6:n_in + 6 + 2 * HG_BLOCKS]
    st_ref, g_ref = refs[-2:]

    @pl.when(pl.program_id(1) == 0)
    def _():
        st_ref[...] = jnp.zeros_like(st_ref)

    C = CHUNK
    for sub in range(HG_SUB):
        k = 0
        for u in range(HG_BLOCKS):
            qf_ref, ff_ref, vf_ref, qb_ref, fb_ref, vb_ref, lb_ref = data[7 * u:7 * u + 7]
            of_ref, ob_ref = outs[2 * u:2 * u + 2]
            dirs = ((qf_ref, ff_ref, vf_ref, trif_ref, tsmf_ref, mf_ref, of_ref, False),
                    (qb_ref, fb_ref, vb_ref, trib_ref, tsmb_ref, mb_ref, ob_ref, True))
            for q_ref, f_ref, v_ref, tri_ref, tsm_ref, m_ref, o_ref, rev in dirs:
                r0 = (HG_SUB - 1 - sub if rev else sub) * C
                rows = slice(r0, r0 + C)
                for hh in range(HG_PAIR):
                    sl = slice(hh * LANE, (hh + 1) * LANE)
                    o = _hg_chain(q_ref[rows, sl], f_ref[rows, sl], v_ref[rows, sl], lb_ref[:, sl],
                                  tri_ref, tsm_ref, m_ref, g_ref.at[k], st_ref.at[k], rev)
                    o_ref[rows, sl] = o.astype(o_ref.dtype)
                    k += 1


def _hgrn(proj, lb, consts_f, consts_b):
    s = proj.shape[0]
    C = HG_SUB * CHUNK
    nc = s // C
    w = HG_PAIR * LANE
    nb = HG_BLOCKS
    const = lambda a: pl.BlockSpec(a.shape, lambda h, c: tuple(0 for _ in a.shape))
    consts = list(consts_f) + list(consts_b)
    in_specs, args, out_specs = [], [], []
    for u in range(nb):
        fwd = lambda off, u=u: pl.BlockSpec((C, w), lambda h, c: (c, off // w + nb * h + u))
        bwd = lambda off, u=u: pl.BlockSpec((C, w), lambda h, c: (nc - 1 - c, off // w + nb * h + u))
        in_specs += [fwd(OFF_Q), fwd(OFF_FF), fwd(OFF_I), bwd(OFF_Q), bwd(OFF_FB), bwd(OFF_I),
                     pl.BlockSpec((1, w), lambda h, c, u=u: (0, nb * h + u))]
        args += [proj] * 6 + [lb]
        out_specs += [pl.BlockSpec((C, w), lambda h, c: (c, h)),
                      pl.BlockSpec((C, w), lambda h, c: (nc - 1 - c, h))]
    n_chain = 2 * nb * HG_PAIR
    outs = pl.pallas_call(
        _hg_kernel,
        grid=(HG_HEADS // (HG_PAIR * nb), nc),
        in_specs=in_specs + [const(a) for a in consts],
        out_specs=out_specs,
        out_shape=[jax.ShapeDtypeStruct((s, D_MODEL // nb), bf16)] * (2 * nb),
        scratch_shapes=[pltpu.VMEM((n_chain, HG_HEAD_DIM, HG_HEAD_DIM), f32),
                        pltpu.VMEM((n_chain, CHUNK, LANE), f32)],
        compiler_params=_params(("parallel", "arbitrary")),
        name="hgrn",
    )(*args, *consts)
    return outs


HG_GATE_COLS = 256


def _hg_out_kernel(*refs):
    nb = HG_BLOCKS
    scan_refs, w_ref = refs[:2 * nb], refs[2 * nb]
    g_refs, y_ref = refs[2 * nb + 1:-1], refs[-1]
    w = HG_PAIR * LANE
    parts = []
    for u in range(nb):
        parts.append(scan_refs[2 * u][...].astype(f32) + scan_refs[2 * u + 1][...].astype(f32))
    cols = [parts[blk % nb][:, (blk // nb) * w:(blk // nb + 1) * w] for blk in range(D_MODEL // w)]
    o = jnp.concatenate(cols, axis=1)
    g = jnp.concatenate([r[...] for r in g_refs], axis=1).astype(f32)
    rs = lax.rsqrt(jnp.mean(o * o, axis=-1, keepdims=True) + RMS_EPS)
    y_ref[...] = (o * rs * w_ref[...] * _silu(g)).astype(y_ref.dtype)


def _hg_out(scans, proj, norm_w):
    s = proj.shape[0]
    d = D_MODEL
    tm = 256
    cw = HG_GATE_COLS
    tile = pl.BlockSpec((tm, d), lambda i: (i, 0))
    part = pl.BlockSpec((tm, d // HG_BLOCKS), lambda i: (i, 0))
    gates = [pl.BlockSpec((tm, cw), functools.partial(lambda i, u: (i, OFF_G // cw + u), u=u))
             for u in range(d // cw)]
    return pl.pallas_call(
        _hg_out_kernel,
        grid=(s // tm,),
        in_specs=[part] * len(scans) + [pl.BlockSpec((1, d), lambda i: (0, 0))] + gates,
        out_specs=tile,
        out_shape=jax.ShapeDtypeStruct((s, d), bf16),
        compiler_params=_params(("parallel",)),
        name="hgrn_out",
    )(*scans, norm_w.reshape(1, d), *([proj] * len(gates)))


def _post_mix_kernel(y_ref, x_ref, gm_ref, wpost_ref, wpre_ref, sc_ref, sh_ref, wr_ref,
                     x1_ref, hp_ref, lg_ref):
    y = y_ref[...]
    yn = y * lax.rsqrt(jnp.mean(y * y, axis=-1, keepdims=True) + RMS_EPS) * wpost_ref[...]
    x1 = x_ref[...] + gm_ref[...] * yn
    x1_ref[...] = x1
    h = x1 * lax.rsqrt(jnp.mean(x1 * x1, axis=-1, keepdims=True) + RMS_EPS) * wpre_ref[...]
    h = h * (1.0 + sc_ref[...]) + sh_ref[...]
    hhi, hlo = _split2(h)
    whi, wlo = _split2(wr_ref[...])
    lg_ref[...] = _dot(hhi, whi) + _dot(hhi, wlo) + _dot(hlo, whi)
    half = h.shape[1] // 2
    top = pltpu.bitcast(hhi[:, :half].astype(f32), jnp.uint32)
    bot = pltpu.bitcast(hhi[:, half:].astype(f32), jnp.uint32)
    hp_ref[...] = top | (bot >> 16)


def _post_mix(y, x, gm, wpost, wpre, sc, sh, wr_pad):
    s, d = x.shape
    tm = 256
    row = pl.BlockSpec((1, d), lambda i: (0, 0))
    tile = pl.BlockSpec((tm, d), lambda i: (i, 0))
    return pl.pallas_call(
        _post_mix_kernel,
        grid=(s // tm,),
        in_specs=[tile, tile, row, row, row, row, row, pl.BlockSpec((d, LANE), lambda i: (0, 0))],
        out_specs=[tile, pl.BlockSpec((tm, d // 2), lambda i: (i, 0)),
                   pl.BlockSpec((tm, LANE), lambda i: (i, 0))],
        out_shape=[jax.ShapeDtypeStruct((s, d), f32), jax.ShapeDtypeStruct((s, d // 2), jnp.uint32),
                   jax.ShapeDtypeStruct((s, LANE), f32)],
        compiler_params=_params(("parallel",), VMEM_LIMIT),
        name="post_mix",
    )(y, x, gm, wpost.reshape(1, d), wpre.reshape(1, d), sc, sh, wr_pad)


def _select_kernel(lg_ref, utri_ref, pos_ref, sel_ref, aff_ref, *, seq, cap):
    E = N_EXPERTS
    nb = seq // LANE
    lg = lg_ref[...]
    lane = lax.broadcasted_iota(jnp.int32, lg.shape, 1)
    lg = jnp.where(lane < E, lg, -jnp.inf)
    mx = jnp.max(lg, axis=-1, keepdims=True)
    ex = jnp.exp(lg - mx)
    aff = ex / jnp.sum(ex, axis=-1, keepdims=True)
    aff_t = aff.T[:E, :]
    aff_ref[...] = aff_t
    bits = pltpu.bitcast(aff_t, jnp.int32)

    def search(i, thr):
        cand = thr | (jnp.int32(1) << (30 - i))
        cnt = jnp.sum((bits >= cand).astype(jnp.int32), axis=-1, keepdims=True)
        return jnp.where(cnt >= cap, cand, thr)

    thr = lax.fori_loop(0, 31, search, jnp.zeros((E, 1), jnp.int32))
    gt = bits > thr
    eq = bits == thr
    need = cap - jnp.sum(gt.astype(jnp.int32), axis=-1, keepdims=True)
    utri = utri_ref[...]

    def cumsum_excl(mask_f):
        carry = jnp.zeros((E, 1), f32)
        outs = []
        for b in range(nb):
            blk = mask_f[:, b * LANE:(b + 1) * LANE]
            inc = _dot(blk.astype(bf16), utri)
            outs.append(inc - blk + carry)
            carry = carry + inc[:, LANE - 1:LANE]
        return jnp.concatenate(outs, axis=1)

    eq_f = eq.astype(f32)
    rank_eq = cumsum_excl(eq_f)
    sel = gt | (eq & (rank_eq < need.astype(f32)))
    sel_f = sel.astype(f32)
    pos = cumsum_excl(sel_f)
    pos_ref[...] = pos
    sel_ref[...] = sel_f


def _select(logits, cap):
    s = logits.shape[0]
    t = np.arange(LANE)
    utri = jnp.asarray((t[:, None] <= t[None, :]).astype(np.float32), bf16)
    full = lambda shape: pl.BlockSpec(shape, lambda: tuple(0 for _ in shape))
    rows = jax.ShapeDtypeStruct((N_EXPERTS, s), f32)
    return pl.pallas_call(
        functools.partial(_select_kernel, seq=s, cap=cap),
        in_specs=[full((s, LANE)), full((LANE, LANE))],
        out_specs=[full((N_EXPERTS, s))] * 3,
        out_shape=[rows, rows, rows],
        compiler_params=_params(None, VMEM_LIMIT),
        name="ec_select",
    )(logits, utri)


def _extract_plan(pos, sel, cap, group):
    incl_end = (pos + sel)[:, LANE - 1::LANE]
    before = jnp.concatenate([jnp.zeros_like(incl_end[:, :1]), incl_end[:, :-1]], axis=1)
    p0 = (jnp.arange(cap // group) * group).astype(f32)
    lo = jnp.sum(incl_end[:, None, :] <= p0[None, :, None], axis=-1)
    hi = jnp.sum(before[:, None, :] < (p0 + group)[None, :, None], axis=-1)
    return lo.astype(jnp.int32), hi.astype(jnp.int32)


def _extract_kernel(lo_ref, hi_ref, pos_ref, sel_ref, aff_ref, idx_ref, gate_ref, *, cap, group):
    e = pl.program_id(0)
    prow = lax.broadcasted_iota(jnp.int32, (group, LANE), 0).astype(f32)
    for c in range(cap // group):
        pcol = prow + float(c * group)
        lo = lo_ref[e, c]

        def body(b, carry):
            cnt, g = carry
            c0 = pl.multiple_of(b * LANE, LANE)
            selr = sel_ref[0, :, pl.ds(c0, LANE)]
            incl = pos_ref[0, :, pl.ds(c0, LANE)] + selr
            affr = aff_ref[0, :, pl.ds(c0, LANE)]
            cnt = cnt + jnp.where(incl <= pcol, 1.0, 0.0)
            g = g + jnp.where((selr > 0.0) & (incl == pcol + 1.0), affr, 0.0)
            return cnt, g

        zero = jnp.zeros((group, LANE), f32)
        cnt, g = lax.fori_loop(lo, hi_ref[e, c], body, (zero, zero))
        idx_col = jnp.sum(cnt, axis=-1, keepdims=True) + (lo * LANE).astype(f32)
        idx_ref[0, c * group:(c + 1) * group, :] = idx_col.astype(jnp.int32)
        gate_ref[0, c * group:(c + 1) * group, :] = jnp.sum(g, axis=-1, keepdims=True)


def _extract(pos, sel, aff, cap):
    s = pos.shape[1]
    group = min(LANE, cap)
    lo, hi = _extract_plan(pos, sel, cap, group)
    full = pl.BlockSpec((1, 1, s), lambda e, *_: (e, 0, 0))
    col = pl.BlockSpec((1, cap, 1), lambda e, *_: (e, 0, 0))
    grid_spec = pltpu.PrefetchScalarGridSpec(
        num_scalar_prefetch=2, grid=(N_EXPERTS,),
        in_specs=[full, full, full], out_specs=[col, col])
    return pl.pallas_call(
        functools.partial(_extract_kernel, cap=cap, group=group),
        grid_spec=grid_spec,
        out_shape=[jax.ShapeDtypeStruct((N_EXPERTS, cap, 1), jnp.int32),
                   jax.ShapeDtypeStruct((N_EXPERTS, cap, 1), f32)],
        compiler_params=_params(("parallel",)),
        name="ec_extract",
    )(lo, hi, *[a.reshape(N_EXPERTS, 1, s) for a in (pos, sel, aff)])


FF_TILE = 256
DOWN_TILE = 512


def _expert_kernel(idx_ref, hp_ref, wg_ref, wu_ref, wd_ref, gate_ref, y_ref,
                   xbuf_ref, xa_ref, xb_ref, hid_ref, sem, *, cap):
    e = pl.program_id(0)
    s = pl.program_id(1)
    n_ff = D_FF_EXPERT // FF_TILE
    half = D_MODEL // 2

    def gather(ex):
        def start(p, c):
            pltpu.make_async_copy(hp_ref.at[pl.ds(idx_ref[ex, p], 1), :],
                                  xbuf_ref.at[pl.ds(p, 1), :], sem).start()
            return c

        lax.fori_loop(0, cap, start, 0)

    @pl.when((s == 0) & (e == 0))
    def _():
        gather(0)

    @pl.when(s == 0)
    def _():
        pltpu.make_async_copy(hp_ref.at[pl.ds(0, cap), :], xbuf_ref, sem).wait()
        u = xbuf_ref[...]
        xa_ref[...] = pltpu.bitcast(u & jnp.uint32(0xFFFF0000), f32).astype(bf16)
        xb_ref[...] = pltpu.bitcast(u << 16, f32).astype(bf16)

    @pl.when((s == 0) & (e + 1 < N_EXPERTS))
    def _():
        gather(e + 1)

    @pl.when(s < n_ff)
    def _():
        wg = wg_ref[0].astype(bf16)
        wu = wu_ref[0].astype(bf16)
        xa = xa_ref[...]
        xb = xb_ref[...]
        hg = _dot(xa, wg[:half]) + _dot(xb, wg[half:])
        hu = _dot(xa, wu[:half]) + _dot(xb, wu[half:])
        c0 = pl.multiple_of(s * FF_TILE, FF_TILE)
        hid_ref[:, pl.ds(c0, FF_TILE)] = (_silu(hg) * hu).astype(bf16)

    @pl.when(s >= n_ff)
    def _():
        y = _dot(hid_ref[...], wd_ref[0].astype(bf16))
        y_ref[0] = (y * gate_ref[0]).astype(y_ref.dtype)


def _experts(idx, hp, w_gate, w_up, w_down, gate, cap):
    n_ff = D_FF_EXPERT // FF_TILE
    n_dn = D_MODEL // DOWN_TILE
    ffj = lambda s: jnp.minimum(s, n_ff - 1)
    dnj = lambda s: jnp.maximum(s - n_ff, 0)
    grid_spec = pltpu.PrefetchScalarGridSpec(
        num_scalar_prefetch=1,
        grid=(N_EXPERTS, n_ff + n_dn),
        in_specs=[pl.BlockSpec(memory_space=pl.ANY),
                  pl.BlockSpec((1, D_MODEL, FF_TILE), lambda e, s, idx: (e, 0, ffj(s))),
                  pl.BlockSpec((1, D_MODEL, FF_TILE), lambda e, s, idx: (e, 0, ffj(s))),
                  pl.BlockSpec((1, D_FF_EXPERT, DOWN_TILE), lambda e, s, idx: (e, 0, dnj(s))),
                  pl.BlockSpec((1, cap, 1), lambda e, s, idx: (e, 0, 0))],
        out_specs=pl.BlockSpec((1, cap, DOWN_TILE), lambda e, s, idx: (e, 0, dnj(s))),
        scratch_shapes=[pltpu.VMEM((cap, D_MODEL // 2), jnp.uint32),
                        pltpu.VMEM((cap, D_MODEL // 2), bf16),
                        pltpu.VMEM((cap, D_MODEL // 2), bf16),
                        pltpu.VMEM((cap, D_FF_EXPERT), bf16),
                        pltpu.SemaphoreType.DMA(())],
    )
    return pl.pallas_call(
        functools.partial(_expert_kernel, cap=cap),
        grid_spec=grid_spec,
        out_shape=jax.ShapeDtypeStruct((N_EXPERTS, cap, D_MODEL), bf16),
        compiler_params=_params(("arbitrary", "arbitrary"), VMEM_LIMIT),
        name="ec_experts",
    )(idx, hp, w_gate, w_up, w_down, gate)


COMBINE_ALIGN = 16


COMBINE_K = 256


def _combine_plan(pos, cap, tt):
    al = COMBINE_ALIGN
    first = pos[:, ::tt].astype(jnp.int32)
    nxt = jnp.concatenate([first[:, 1:], jnp.full((first.shape[0], 1), cap, jnp.int32)], axis=1)
    start = first // al * al
    pieces = jnp.where(nxt > first, (nxt - start + al - 1) // al, 0)
    base = (jnp.cumsum(pieces, axis=0) - pieces) * al
    total = jnp.sum(pieces, axis=0)
    chunks = (total * al + COMBINE_K - 1) // COMBINE_K
    return start, pieces, base, total, chunks


def _combine_kernel(start_ref, pieces_ref, base_ref, total_ref, chunks_ref,
                    y_ref, pos_ref, sel_ref, x1_ref, gf_ref, w_ref, o_ref,
                    stage_ref, acc_ref, sem, *, tt):
    i = pl.program_id(0)
    al = COMBINE_ALIGN

    @pl.when(i == 0)
    def _():
        stage_ref[...] = jnp.zeros_like(stage_ref)

    for e in range(N_EXPERTS):
        def issue(j, c):
            src = pl.multiple_of(start_ref[e, i] + j * al, al)
            dst = pl.multiple_of(base_ref[e, i] + j * al, al)
            pltpu.make_async_copy(y_ref.at[e, pl.ds(src, al), :], stage_ref.at[pl.ds(dst, al), :], sem).start()
            return c

        lax.fori_loop(0, pieces_ref[e, i], issue, 0)

    keys = []
    for e in range(N_EXPERTS):
        shift = (base_ref[e, i] - start_ref[e, i]).astype(f32)
        keys.append(jnp.where(sel_ref[e:e + 1, :] > 0.0, pos_ref[e:e + 1, :] + shift, -1.0))

    def wait(j, c):
        pltpu.make_async_copy(y_ref.at[0, pl.ds(0, al), :], stage_ref.at[pl.ds(0, al), :], sem).wait()
        return c

    lax.fori_loop(0, total_ref[i], wait, 0)
    acc_ref[...] = jnp.zeros_like(acc_ref)
    riota = lax.broadcasted_iota(jnp.int32, (COMBINE_K, tt), 0).astype(f32)

    def chunk(k, c):
        r0 = pl.multiple_of(k * COMBINE_K, COMBINE_K)
        rows = riota + r0.astype(f32)
        et = jnp.zeros((COMBINE_K, tt), f32)
        for e in range(N_EXPERTS):
            et = jnp.where(keys[e] == rows, 1.0, et)
        acc_ref[...] += lax.dot_general(et.astype(bf16), stage_ref[pl.ds(r0, COMBINE_K), :],
                                        (((0,), (0,)), ((), ())), preferred_element_type=f32)
        return c

    lax.fori_loop(0, chunks_ref[i], chunk, 0)
    y = acc_ref[...]
    yn = y * lax.rsqrt(jnp.mean(y * y, axis=-1, keepdims=True) + RMS_EPS) * w_ref[...]
    o_ref[...] = x1_ref[...] + gf_ref[...] * yn


def _combine(plan, yexp, pos, sel, x1, gf, w_post, tt):
    s, d = x1.shape
    stage_rows = N_EXPERTS * (tt + 2 * COMBINE_ALIGN)
    stage_rows = (stage_rows + COMBINE_K - 1) // COMBINE_K * COMBINE_K
    row = pl.BlockSpec((1, d), lambda i, *_: (0, 0))
    grid_spec = pltpu.PrefetchScalarGridSpec(
        num_scalar_prefetch=5,
        grid=(s // tt,),
        in_specs=[pl.BlockSpec(memory_space=pl.ANY),
                  pl.BlockSpec((N_EXPERTS, tt), lambda i, *_: (0, i)),
                  pl.BlockSpec((N_EXPERTS, tt), lambda i, *_: (0, i)),
                  pl.BlockSpec((tt, d), lambda i, *_: (i, 0)), row, row],
        out_specs=pl.BlockSpec((tt, d), lambda i, *_: (i, 0)),
        scratch_shapes=[pltpu.VMEM((stage_rows, d), bf16), pltpu.VMEM((tt, d), f32),
                        pltpu.SemaphoreType.DMA(())],
    )
    return pl.pallas_call(
        functools.partial(_combine_kernel, tt=tt),
        grid_spec=grid_spec,
        out_shape=jax.ShapeDtypeStruct((s, d), f32),
        compiler_params=_params(("arbitrary",), VMEM_LIMIT),
        name="ec_combine",
    )(*plan, yexp, pos, sel, x1, gf, w_post.reshape(1, d))


GATE_SUB = 256


def _gate_extras(proj, off, tm, tn):
    n_sub = tn // GATE_SUB
    return [(proj, (tm, GATE_SUB), functools.partial(
        lambda m, j, u: (m, off // GATE_SUB + j * n_sub + u), u=u)) for u in range(n_sub)]


def _merge_a(acc, *gate_refs):
    ga = jnp.concatenate([r[...] for r in gate_refs], axis=1).astype(f32)
    return _sigmoid(ga) * acc


def _merge_b(acc, part_ref, *gate_refs):
    gb = jnp.concatenate([r[...] for r in gate_refs], axis=1).astype(f32)
    return part_ref[...] + _sigmoid(gb) * acc


def _layer(x, mod, lb, norm_pre_mix, norm_post_mix, norm_pre_ffn, norm_post_ffn,
           w_in, conv_w, conv_b, dt_bias, a_log, d_skip, ssd_norm_w, hg_norm_w,
           w_ssd_out, w_hg_out, w_mix_out, w_router, w_gate, w_up, w_down):
    s, d = x.shape
    sh_m, sc_m, g_m, sh_f, sc_f, g_f = [mod[:, i * d:(i + 1) * d] for i in range(6)]

    h = _prenorm(x, norm_pre_mix, sc_m, sh_m)
    tm8, tm16 = s // 8, s // 16
    proj = _wsmm(h, w_in, tm=tm16, tn=1280, out_dtype=bf16, name="in_proj")

    xact = _conv(proj, conv_w, conv_b)
    tri_f, tri_b, sel_a, sel_w, ea, ew = _ssd_consts()
    cb = lambda a: jnp.asarray(a, bf16)
    paf, pwf, arf, pab, pwb, arb = _ssd_prep(proj, dt_bias, a_log,
                                             (cb(tri_f), cb(tri_b), cb(sel_a), cb(sel_w)))
    y_f = _ssd(xact, paf, pwf, arf, cb(ea), cb(ew), rev=False)
    dskip = jnp.repeat(d_skip, SSD_HEAD_DIM).reshape(1, SSD_D_INNER)
    y_ssd = _ssd(xact, pab, pwb, arb, cb(ea), cb(ew), rev=True,
                 final_args=(y_f, proj, ssd_norm_w.reshape(1, SSD_D_INNER), dskip))

    hg_consts = [tuple((cb(tri), cb(tsm), jnp.asarray(msk))) for tri, tsm, msk in
                 (_hg_consts(False), _hg_consts(True))]
    scans = _hgrn(proj, lb, hg_consts[0], hg_consts[1])
    y_hg = _hg_out(scans, proj, hg_norm_w)

    part = _wsmm(y_ssd, w_ssd_out, tm=tm16, tn=512, out_dtype=f32,
                 extras=_gate_extras(proj, OFF_GA, tm16, 512),
                 epilogue=_merge_a, name="ssd_out_proj")
    merged = _wsmm(y_hg, w_hg_out, tm=tm8, tn=512, out_dtype=bf16,
                   extras=[(part, (tm8, 512), lambda m, j: (m, j))]
                   + _gate_extras(proj, OFF_GB, tm8, 512),
                   epilogue=_merge_b, name="hg_out_proj")
    y_mix = _wsmm(merged, w_mix_out, tm=tm8, tn=512, out_dtype=f32, name="mix_out_proj")

    wr_pad = jnp.pad(w_router, ((0, 0), (0, LANE - N_EXPERTS)))
    x1, hp, logits = _post_mix(y_mix, x, g_m, norm_post_mix, norm_pre_ffn, sc_f, sh_f, wr_pad)
    cap = 2 * s // N_EXPERTS
    pos, sel, aff = _select(logits, cap)
    idx, gate = _extract(pos, sel, aff, cap)
    yexp = _experts(idx.reshape(N_EXPERTS, cap), hp, w_gate, w_up, w_down, gate, cap)
    tt = min(128, cap // 2)
    plan = _combine_plan(pos, cap, tt)
    return _combine(plan, yexp, pos, sel, x1, g_f, norm_post_ffn, tt)


def kernel(x, c, w_ada, b_ada, norm_pre_mix, norm_post_mix, norm_pre_ffn, norm_post_ffn, w_in, conv_w, conv_b, dt_bias_fwd, dt_bias_bwd, a_log_fwd, a_log_bwd, d_skip, ssd_norm_w, hg_lower_bound, hg_norm_w, w_ssd_out, w_hg_out, w_mix_out, w_router, w_gate, w_up, w_down):
    depth = w_ada.shape[0]
    lower_bounds = jnp.cumsum(jax.nn.softmax(hg_lower_bound.astype(f32), axis=0), axis=0)
    outs = []
    for bi in range(x.shape[0]):
        xb = x[bi]
        cb_ = c[bi:bi + 1]
        for l in range(depth):
            mod = _ada(cb_, w_ada[l], b_ada[l])
            dt_bias = jnp.concatenate([dt_bias_fwd[l], dt_bias_bwd[l]]).reshape(1, -1)
            a_log = jnp.concatenate([a_log_fwd[l], a_log_bwd[l]]).reshape(1, -1)
            xb = _layer(xb, mod, lower_bounds[l].reshape(1, -1), norm_pre_mix[l], norm_post_mix[l],
                        norm_pre_ffn[l], norm_post_ffn[l], w_in[l], conv_w[l], conv_b[l],
                        dt_bias, a_log, d_skip[l], ssd_norm_w[l], hg_norm_w[l],
                        w_ssd_out[l], w_hg_out[l], w_mix_out[l], w_router[l],
                        w_gate[l], w_up[l], w_down[l])
        outs.append(xb)
    return jnp.stack(outs, axis=0)
```

```python
import functools

import numpy as np
import jax
import jax.numpy as jnp
from jax import lax
from jax.experimental import pallas as pl
from jax.experimental.pallas import tpu as pltpu

f32 = jnp.float32
bf16 = jnp.bfloat16

D_MODEL = 4096
SSD_D_INNER = 8192
SSD_HEADS = 128
SSD_HEAD_DIM = 64
SSD_D_STATE = 128
SSD_GROUPS = 8
SSD_HEADS_PER_GROUP = 16
SSD_CONV = 5
SSD_XBC = SSD_D_INNER + 2 * SSD_GROUPS * SSD_D_STATE
HG_HEADS = 32
HG_HEAD_DIM = 128
N_EXPERTS = 16
D_FF_EXPERT = 2048
RMS_EPS = 1e-6

OFF_Z = 0
OFF_XBC = OFF_Z + SSD_D_INNER
OFF_DT = OFF_XBC + SSD_XBC
OFF_Q = OFF_DT + 2 * SSD_HEADS
OFF_FF = OFF_Q + D_MODEL
OFF_FB = OFF_FF + D_MODEL
OFF_I = OFF_FB + D_MODEL
OFF_G = OFF_I + D_MODEL
OFF_GA = OFF_G + D_MODEL
OFF_GB = OFF_GA + D_MODEL
D_IN_PROJ = OFF_GB + D_MODEL

LOG2E = 1.4426950408889634
LANE = 128
CHUNK = 128
VMEM_LIMIT = 56 * 1024 * 1024


def _params(sem, vmem=None):
    return pltpu.CompilerParams(dimension_semantics=sem, vmem_limit_bytes=vmem)


def _sigmoid(x):
    return 1.0 / (1.0 + jnp.exp(-x))


def _silu(x):
    return x * _sigmoid(x)


def _softplus(x):
    return jnp.maximum(x, 0.0) + jnp.log(1.0 + jnp.exp(-jnp.abs(x)))


def _split2(x):
    hi = x.astype(bf16)
    lo = (x - hi.astype(f32)).astype(bf16)
    return hi, lo


def _split3(x):
    hi = x.astype(bf16)
    r = x - hi.astype(f32)
    mid = r.astype(bf16)
    lo = (r - mid.astype(f32)).astype(bf16)
    return hi, mid, lo


def _dot(a, b):
    return jnp.dot(a, b, preferred_element_type=f32)


def _dot_nt(a, b):
    return lax.dot_general(a, b, (((1,), (1,)), ((), ())), preferred_element_type=f32)


def _ada_kernel(c_ref, w_ref, b_ref, o_ref):
    c = c_ref[...]
    ca = jnp.broadcast_to(_silu(c), (8, c.shape[1]))
    chi, clo = _split2(ca)
    whi, wlo = _split2(w_ref[...])
    acc = _dot(chi, whi) + _dot(chi, wlo) + _dot(clo, whi)
    o_ref[...] = acc[0:1, :] + b_ref[...]


def _ada(c, w, b):
    d, n = w.shape
    tn = 512
    return pl.pallas_call(
        _ada_kernel,
        grid=(n // tn,),
        in_specs=[pl.BlockSpec((1, d), lambda j: (0, 0)),
                  pl.BlockSpec((d, tn), lambda j: (0, j)),
                  pl.BlockSpec((1, tn), lambda j: (0, j))],
        out_specs=pl.BlockSpec((1, tn), lambda j: (0, j)),
        out_shape=jax.ShapeDtypeStruct((1, n), f32),
        compiler_params=_params(("parallel",), VMEM_LIMIT),
        name="adaln",
    )(c, w, b.reshape(1, n))


def _prenorm_kernel(x_ref, w_ref, sc_ref, sh_ref, o_ref):
    x = x_ref[...]
    r = lax.rsqrt(jnp.mean(x * x, axis=-1, keepdims=True) + RMS_EPS)
    h = (x * r * w_ref[...]) * (1.0 + sc_ref[...]) + sh_ref[...]
    o_ref[...] = h.astype(o_ref.dtype)


def _prenorm(x, w, sc, sh):
    s, d = x.shape
    tm = 256
    row = pl.BlockSpec((1, d), lambda i: (0, 0))
    return pl.pallas_call(
        _prenorm_kernel,
        grid=(s // tm,),
        in_specs=[pl.BlockSpec((tm, d), lambda i: (i, 0)), row, row, row],
        out_specs=pl.BlockSpec((tm, d), lambda i: (i, 0)),
        out_shape=jax.ShapeDtypeStruct((s, d), bf16),
        compiler_params=_params(("parallel",)),
        name="prenorm",
    )(x, w.reshape(1, d), sc, sh)


def _wsmm_kernel(*refs, n_extra, nn, epilogue):
    a_ref, w_ref = refs[0], refs[1]
    extras = refs[2:2 + n_extra]
    o_ref = refs[2 + n_extra]
    wbuf_ref = refs[3 + n_extra]
    n = pl.program_id(0)
    m = pl.program_id(1)
    kc = w_ref.shape[0]

    @pl.when(n < nn)
    def _():
        r0 = pl.multiple_of(m * kc, kc)
        wbuf_ref[n % 2, pl.ds(r0, kc), :] = w_ref[...].astype(bf16)

    @pl.when(n == 0)
    def _():
        o_ref[...] = jnp.zeros_like(o_ref)

    @pl.when(n > 0)
    def _():
        acc = _dot(a_ref[...], wbuf_ref[(n - 1) % 2])
        o_ref[...] = epilogue(acc, *extras).astype(o_ref.dtype)


def _wsmm(a, w, *, tm, tn, out_dtype, extras=(), epilogue=None, name="matmul"):
    m, kdim = a.shape
    n = w.shape[1]
    nm, nn = m // tm, n // tn
    kc = kdim // nm
    if epilogue is None:
        epilogue = lambda acc: acc
    jn = lambda nidx: jnp.maximum(nidx - 1, 0)
    extra_arrays = [e[0] for e in extras]
    extra_specs = [pl.BlockSpec(e[1], functools.partial(lambda nidx, midx, f: f(midx, jn(nidx)), f=e[2]))
                   for e in extras]
    return pl.pallas_call(
        functools.partial(_wsmm_kernel, n_extra=len(extras), nn=nn, epilogue=epilogue),
        grid=(nn + 1, nm),
        in_specs=[pl.BlockSpec((tm, kdim), lambda nidx, midx: (jnp.where(nidx == 0, 0, midx), 0)),
                  pl.BlockSpec((kc, tn), lambda nidx, midx: (midx, jnp.minimum(nidx, nn - 1)))] + extra_specs,
        out_specs=pl.BlockSpec((tm, tn), lambda nidx, midx: (jnp.where(nidx == 0, 0, midx), jn(nidx))),
        out_shape=jax.ShapeDtypeStruct((m, n), out_dtype),
        scratch_shapes=[pltpu.VMEM((2, kdim, tn), bf16)],
        compiler_params=_params(("arbitrary", "arbitrary"), VMEM_LIMIT),
        name=name,
    )(a, w, *extra_arrays)


CONV_HALO = 16


def _conv_kernel(x_ref, w_ref, b_ref, o_ref, *, seq, tile):
    w = w_ref[...]
    b = b_ref[...]
    nt = seq // tile
    ext_rows = tile + 2 * CONV_HALO

    def body(i, carry):
        r0 = pl.multiple_of(i * tile, tile)
        cur = x_ref[pl.ds(r0, tile), :].astype(f32)
        p0 = pl.multiple_of(jnp.maximum(r0 - CONV_HALO, 0), CONV_HALO)
        n0 = pl.multiple_of(jnp.minimum(r0 + tile, seq - CONV_HALO), CONV_HALO)
        prev = jnp.where(i > 0, x_ref[pl.ds(p0, CONV_HALO), :].astype(f32), 0.0)
        nxt = jnp.where(i < nt - 1, x_ref[pl.ds(n0, CONV_HALO), :].astype(f32), 0.0)
        ext = jnp.concatenate([prev, cur, nxt], axis=0)
        acc = b + w[2:3, :] * cur
        for k in (0, 1, 3, 4):
            d = k - SSD_CONV // 2
            shifted = pltpu.roll(ext, (-d) % ext_rows, axis=0)[CONV_HALO:CONV_HALO + tile, :]
            acc = acc + w[k:k + 1, :] * shifted
        o_ref[pl.ds(r0, tile), :] = _silu(acc).astype(o_ref.dtype)
        return carry

    lax.fori_loop(0, nt, body, 0)


def _conv(proj, conv_w, conv_b):
    s = proj.shape[0]
    cw = 256
    off = OFF_XBC // cw
    tile = min(512, s)
    return pl.pallas_call(
        functools.partial(_conv_kernel, seq=s, tile=tile),
        grid=(SSD_XBC // cw,),
        in_specs=[pl.BlockSpec((s, cw), lambda j: (0, off + j)),
                  pl.BlockSpec((SSD_CONV, cw), lambda j: (0, j)),
                  pl.BlockSpec((1, cw), lambda j: (0, j))],
        out_specs=pl.BlockSpec((s, cw), lambda j: (0, j)),
        out_shape=jax.ShapeDtypeStruct((s, SSD_XBC), bf16),
        compiler_params=_params(("parallel",), VMEM_LIMIT),
        name="conv_silu",
    )(proj, conv_w, conv_b.reshape(1, SSD_XBC))


def _ssd_consts():
    L = CHUNK
    t = np.arange(L)
    tri_f = (t[None, :] <= t[:, None]).astype(np.float32)
    tri_b = (t[None, :] >= t[:, None]).astype(np.float32)
    h = np.arange(SSD_HEADS)
    g, r = h // SSD_HEADS_PER_GROUP, h % SSD_HEADS_PER_GROUP
    sel_a = np.zeros((3 * SSD_HEADS, SSD_GROUPS * LANE), np.float32)
    for j in range(3):
        sel_a[j * SSD_HEADS + h, g * LANE + j * SSD_HEADS_PER_GROUP + r] = 1.0
    ea = np.zeros((LANE, SSD_HEADS_PER_GROUP * L), np.float32)
    for j in range(3):
        for rr in range(SSD_HEADS_PER_GROUP):
            ea[j * SSD_HEADS_PER_GROUP + rr, rr * L:(rr + 1) * L] = 1.0
    return tri_f, tri_b, sel_a, ea


def _ssd_prep_kernel(dt_ref, bias_ref, alog_ref, trif_ref, trib_ref, sela_ref, *out_refs):
    raw = dt_ref[...].astype(f32)
    L = raw.shape[0]
    for d, (tri_ref, tot_row) in enumerate(((trif_ref, L - 1), (trib_ref, 0))):
        pa_ref, ar_ref, dtr_ref, wr_ref = out_refs[4 * d:4 * d + 4]
        sl = slice(d * SSD_HEADS, (d + 1) * SSD_HEADS)
        dt = _softplus(raw[:, sl] + bias_ref[:, sl])
        adt = dt * (-jnp.exp(alog_ref[:, sl])) * LOG2E
        h3 = jnp.concatenate(_split3(adt), axis=1)
        a3 = _dot(tri_ref[...], h3)
        a = (a3[:, :SSD_HEADS] + a3[:, SSD_HEADS:2 * SSD_HEADS]) + a3[:, 2 * SSD_HEADS:]
        atot = a[tot_row:tot_row + 1, :]
        ar_ref[...] = a.T
        dtr_ref[...] = dt.T
        wr_ref[...] = (jnp.exp2(atot - a) * dt).T
        pa_ref[...] = _dot(jnp.concatenate(_split3(a), axis=1), sela_ref[...]).astype(bf16)


def _ssd_prep(proj, dt_bias, a_log, consts):
    s = proj.shape[0]
    L = CHUNK
    tri_f, tri_b, sel_a = consts
    gl = SSD_GROUPS * LANE
    const = lambda shape: pl.BlockSpec(shape, lambda c: (0, 0))
    packed = pl.BlockSpec((L, gl), lambda c: (c, 0))
    rowf = pl.BlockSpec((SSD_HEADS, L), lambda c: (0, c))
    row_shape = jax.ShapeDtypeStruct((SSD_HEADS, s), f32)
    return pl.pallas_call(
        _ssd_prep_kernel,
        grid=(s // L,),
        in_specs=[pl.BlockSpec((L, 2 * SSD_HEADS), lambda c: (c, OFF_DT // (2 * SSD_HEADS))),
                  const((1, 2 * SSD_HEADS)), const((1, 2 * SSD_HEADS)),
                  const((L, L)), const((L, L)), const(sel_a.shape)],
        out_specs=[packed, rowf, rowf, rowf] * 2,
        out_shape=[jax.ShapeDtypeStruct((s, gl), bf16), row_shape, row_shape, row_shape] * 2,
        compiler_params=_params(("parallel",)),
        name="ssd_prep",
    )(proj, dt_bias, a_log, tri_f, tri_b, sel_a)


def _ssd_kernel(*refs, rev, final):
    if final:
        (x_ref, b_ref, c_ref, pa_ref, ar_ref, dtr_ref, wr_ref, ea_ref,
         yprev_ref, z_ref, nw_ref, dsk_ref, o_ref, st_ref) = refs
    else:
        x_ref, b_ref, c_ref, pa_ref, ar_ref, dtr_ref, wr_ref, ea_ref, o_ref, st_ref = refs
    L = CHUNK
    npair = SSD_HEADS_PER_GROUP // 2

    @pl.when(pl.program_id(1) == 0)
    def _():
        st_ref[...] = jnp.zeros_like(st_ref)

    x = x_ref[...]
    bm = b_ref[...]
    cm = c_ref[...]
    cf = cm.astype(f32)
    aexp = _dot(pa_ref[...], ea_ref[...])
    tot_row = 0 if rev else L - 1
    btf = bm.astype(f32).T
    cb = _dot(cm, btf.astype(bf16))
    ti = lax.broadcasted_iota(jnp.int32, (L, L), 0)
    si = lax.broadcasted_iota(jnp.int32, (L, L), 1)
    keep = (si >= ti) if rev else (si <= ti)
    cbm = jnp.where(keep, cb, 0.0)
    lane = lax.broadcasted_iota(jnp.int32, (L, LANE), 1)
    first = lane < SSD_HEAD_DIM
    ys = []
    for j in range(npair):
        dmats, cmats, bmats, decs = [], [], [], []
        for r in (2 * j, 2 * j + 1):
            ae = aexp[:, r * L:(r + 1) * L]
            diff = ae - ar_ref[r:r + 1, :]
            dmats.append((jnp.exp2(jnp.minimum(diff, 0.0)) * cbm * dtr_ref[r:r + 1, :]).astype(bf16))
            cmats.append((jnp.exp2(ae) * cf).astype(bf16))
            bmats.append((btf * wr_ref[r:r + 1, :]).astype(bf16))
            decs.append(jnp.exp2(ae[tot_row:tot_row + 1, :]))
        xp = x[:, j * LANE:(j + 1) * LANE]
        zero = jnp.zeros_like(xp)
        xbd = jnp.concatenate([jnp.where(first, xp, zero), jnp.where(first, zero, xp)], axis=0)
        st = st_ref[j]
        stb = st.astype(bf16)
        stbd = jnp.concatenate([jnp.where(first, stb, zero), jnp.where(first, zero, stb)], axis=0)
        y = _dot(jnp.concatenate(dmats + cmats, axis=1), jnp.concatenate([xbd, stbd], axis=0))
        ys.append(y)
        dec = jnp.where(first[:1, :], decs[0], decs[1])
        st_ref[j] = st * dec + _dot(jnp.concatenate(bmats, axis=1), xbd)
    y = jnp.concatenate(ys, axis=1)
    if final:
        y = y + yprev_ref[...].astype(f32) + dsk_ref[...] * x.astype(f32)
        y = y * _silu(z_ref[...].astype(f32))
        y = y * lax.rsqrt(jnp.mean(y * y, axis=-1, keepdims=True) + RMS_EPS) * nw_ref[...]
    o_ref[...] = y.astype(o_ref.dtype)


def _ssd(xact, pa, arow, dtrow, wrow, ea, *, rev, final_args=None):
    s = xact.shape[0]
    L = CHUNK
    nc = s // L
    gw = SSD_HEADS_PER_GROUP * SSD_HEAD_DIM
    cc = (lambda c: nc - 1 - c) if rev else (lambda c: c)
    nxg = SSD_D_INNER // LANE
    rowf = pl.BlockSpec((SSD_HEADS_PER_GROUP, L), lambda g, c: (g, cc(c)))
    in_specs = [pl.BlockSpec((L, gw), lambda g, c: (cc(c), g)),
                pl.BlockSpec((L, LANE), lambda g, c: (cc(c), nxg + g)),
                pl.BlockSpec((L, LANE), lambda g, c: (cc(c), nxg + SSD_GROUPS + g)),
                pl.BlockSpec((L, LANE), lambda g, c: (cc(c), g)),
                rowf, rowf, rowf,
                pl.BlockSpec(ea.shape, lambda g, c: (0, 0))]
    args = [xact, xact, xact, pa, arow, dtrow, wrow, ea]
    final = final_args is not None
    if final:
        yprev, proj, norm_w, dskip = final_args
        in_specs += [pl.BlockSpec((L, gw), lambda g, c: (cc(c), g)),
                     pl.BlockSpec((L, gw), lambda g, c: (cc(c), OFF_Z // gw + g)),
                     pl.BlockSpec((1, gw), lambda g, c: (0, g)),
                     pl.BlockSpec((1, gw), lambda g, c: (0, g))]
        args += [yprev, proj, norm_w, dskip]
    return pl.pallas_call(
        functools.partial(_ssd_kernel, rev=rev, final=final),
        grid=(SSD_GROUPS, nc),
        in_specs=in_specs,
        out_specs=pl.BlockSpec((L, gw), lambda g, c: (cc(c), g)),
        out_shape=jax.ShapeDtypeStruct((s, SSD_D_INNER), bf16),
        scratch_shapes=[pltpu.VMEM((SSD_HEADS_PER_GROUP // 2, SSD_D_STATE, LANE), f32)],
        compiler_params=_params(("parallel", "arbitrary")),
        name="ssd_bwd" if rev else "ssd_fwd",
    )(*args)


HG_LEVELS = 8
HG_MM_LEVELS = 2
HG_PAIR = 2
HG_BLOCKS = 2
HG_SUB = 4


def _hg_level_ref_row(lv, blk, rev):
    b = 1 << (lv - 1)
    return blk * 2 * b + (b if rev else b - 1)


def _hg_consts(rev):
    C = CHUNK
    t = np.arange(C)
    if rev:
        tri = (t[None, :] >= t[:, None]).astype(np.float32)
    else:
        tri = (t[None, :] <= t[:, None]).astype(np.float32)
    tmats = []
    masks = [(t[:, None] == t[None, :]).astype(np.float32)]
    for lv in range(1, HG_LEVELS):
        b = 1 << (lv - 1)
        blk = t // (2 * b)
        upper = (t % (2 * b)) >= b
        if rev:
            mask = (blk[:, None] == blk[None, :]) & (~upper)[:, None] & upper[None, :]
        else:
            mask = (blk[:, None] == blk[None, :]) & upper[:, None] & (~upper)[None, :]
        if lv <= HG_MM_LEVELS:
            tmats.append(tri - tri[_hg_level_ref_row(lv, blk, rev)])
        masks.append(mask.astype(np.float32))
    return tri, np.concatenate(tmats, axis=0), np.stack(masks, axis=0)


def _hg_chain(qb, fb, vb, lb, tri_ref, tsm_ref, m_ref, g_ref, st_ref, rev):
    C = CHUNK
    qf = qb.astype(f32)
    fr = fb.astype(f32)
    sg = _sigmoid(fr)
    lf = jnp.log2(lb + (1.0 - lb) * sg)
    kk = (1.0 - lb) * (1.0 - sg)
    kb = kk.astype(bf16)
    h3 = _split3(lf)
    g3 = _dot(tri_ref[...], jnp.concatenate(h3, axis=1))
    g = (g3[:, :LANE] + g3[:, LANE:2 * LANE]) + g3[:, 2 * LANE:]
    e2 = _dot(tsm_ref[...], jnp.concatenate(h3[:2], axis=1))
    e_small = e2[:, :LANE] + e2[:, LANE:]
    g_ref[...] = g
    sc = _dot_nt(qb, kb) * m_ref[0]
    for lv in range(1, HG_LEVELS):
        if lv <= HG_MM_LEVELS:
            e = e_small[(lv - 1) * C:lv * C, :]
        else:
            b = 1 << (lv - 1)
            rows = []
            for blk in range(C // (2 * b)):
                m = _hg_level_ref_row(lv, blk, rev)
                rows.append(jnp.broadcast_to(g_ref[m:m + 1, :], (2 * b, LANE)))
            e = g - jnp.concatenate(rows, axis=0)
        xd = jnp.exp2(-jnp.abs(e)).astype(bf16)
        sc = sc + _dot_nt(qb * xd, kb * xd) * m_ref[lv]
    o = _dot(sc.astype(bf16), vb)
    st = st_ref[...]
    o = o + _dot_nt((qf * jnp.exp2(g)).astype(bf16), st.astype(bf16))
    tot_row = 0 if rev else C - 1
    gtot = g[tot_row:tot_row + 1, :]
    kdec = (kk * jnp.exp2(gtot - g)).astype(bf16)
    vt = vb.astype(f32).T.astype(bf16)
    st_ref[...] = st * jnp.exp2(gtot) + _dot(vt, kdec)
    return o


def _hg_kernel(*refs):
    n_in = 7 * HG_BLOCKS
    data = refs[:n_in]
    trif_ref, tsmf_ref, mf_ref, trib_ref, tsmb_ref, mb_ref = refs[n_in:n_in + 6]
    outs = refs[n_in + 6:n_in + 6 + 2 * HG_BLOCKS]
    st_ref, g_ref = refs[-2:]

    @pl.when(pl.program_id(1) == 0)
    def _():
        st_ref[...] = jnp.zeros_like(st_ref)

    C = CHUNK
    for sub in range(HG_SUB):
        k = 0
        for u in range(HG_BLOCKS):
            qf_ref, ff_ref, vf_ref, qb_ref, fb_ref, vb_ref, lb_ref = data[7 * u:7 * u + 7]
            of_ref, ob_ref = outs[2 * u:2 * u + 2]
            dirs = ((qf_ref, ff_ref, vf_ref, trif_ref, tsmf_ref, mf_ref, of_ref, False),
                    (qb_ref, fb_ref, vb_ref, trib_ref, tsmb_ref, mb_ref, ob_ref, True))
            for q_ref, f_ref, v_ref, tri_ref, tsm_ref, m_ref, o_ref, rev in dirs:
                r0 = (HG_SUB - 1 - sub if rev else sub) * C
                rows = slice(r0, r0 + C)
                for hh in range(HG_PAIR):
                    sl = slice(hh * LANE, (hh + 1) * LANE)
                    o = _hg_chain(q_ref[rows, sl], f_ref[rows, sl], v_ref[rows, sl], lb_ref[:, sl],
                                  tri_ref, tsm_ref, m_ref, g_ref.at[k], st_ref.at[k], rev)
                    o_ref[rows, sl] = o.astype(o_ref.dtype)
                    k += 1


def _hgrn(proj, lb, consts_f, consts_b):
    s = proj.shape[0]
    C = HG_SUB * CHUNK
    nc = s // C
    w = HG_PAIR * LANE
    nb = HG_BLOCKS
    const = lambda a: pl.BlockSpec(a.shape, lambda h, c: tuple(0 for _ in a.shape))
    consts = list(consts_f) + list(consts_b)
    in_specs, args, out_specs = [], [], []
    for u in range(nb):
        fwd = lambda off, u=u: pl.BlockSpec((C, w), lambda h, c: (c, off // w + nb * h + u))
        bwd = lambda off, u=u: pl.BlockSpec((C, w), lambda h, c: (nc - 1 - c, off // w + nb * h + u))
        in_specs += [fwd(OFF_Q), fwd(OFF_FF), fwd(OFF_I), bwd(OFF_Q), bwd(OFF_FB), bwd(OFF_I),
                     pl.BlockSpec((1, w), lambda h, c, u=u: (0, nb * h + u))]
        args += [proj] * 6 + [lb]
        out_specs += [pl.BlockSpec((C, w), lambda h, c: (c, h)),
                      pl.BlockSpec((C, w), lambda h, c: (nc - 1 - c, h))]
    n_chain = 2 * nb * HG_PAIR
    outs = pl.pallas_call(
        _hg_kernel,
        grid=(HG_HEADS // (HG_PAIR * nb), nc),
        in_specs=in_specs + [const(a) for a in consts],
        out_specs=out_specs,
        out_shape=[jax.ShapeDtypeStruct((s, D_MODEL // nb), bf16)] * (2 * nb),
        scratch_shapes=[pltpu.VMEM((n_chain, HG_HEAD_DIM, HG_HEAD_DIM), f32),
                        pltpu.VMEM((n_chain, CHUNK, LANE), f32)],
        compiler_params=_params(("parallel", "arbitrary")),
        name="hgrn",
    )(*args, *consts)
    return outs


HG_GATE_COLS = 256


def _hg_out_kernel(*refs):
    nb = HG_BLOCKS
    scan_refs, w_ref = refs[:2 * nb], refs[2 * nb]
    g_refs, y_ref = refs[2 * nb + 1:-1], refs[-1]
    w = HG_PAIR * LANE
    parts = []
    for u in range(nb):
        parts.append(scan_refs[2 * u][...].astype(f32) + scan_refs[2 * u + 1][...].astype(f32))
    cols = [parts[blk % nb][:, (blk // nb) * w:(blk // nb + 1) * w] for blk in range(D_MODEL // w)]
    o = jnp.concatenate(cols, axis=1)
    g = jnp.concatenate([r[...] for r in g_refs], axis=1).astype(f32)
    rs = lax.rsqrt(jnp.mean(o * o, axis=-1, keepdims=True) + RMS_EPS)
    y_ref[...] = (o * rs * w_ref[...] * _silu(g)).astype(y_ref.dtype)


def _hg_out(scans, proj, norm_w):
    s = proj.shape[0]
    d = D_MODEL
    tm = 256
    cw = HG_GATE_COLS
    tile = pl.BlockSpec((tm, d), lambda i: (i, 0))
    part = pl.BlockSpec((tm, d // HG_BLOCKS), lambda i: (i, 0))
    gates = [pl.BlockSpec((tm, cw), functools.partial(lambda i, u: (i, OFF_G // cw + u), u=u))
             for u in range(d // cw)]
    return pl.pallas_call(
        _hg_out_kernel,
        grid=(s // tm,),
        in_specs=[part] * len(scans) + [pl.BlockSpec((1, d), lambda i: (0, 0))] + gates,
        out_specs=tile,
        out_shape=jax.ShapeDtypeStruct((s, d), bf16),
        compiler_params=_params(("parallel",)),
        name="hgrn_out",
    )(*scans, norm_w.reshape(1, d), *([proj] * len(gates)))


def _post_mix_kernel(y_ref, x_ref, gm_ref, wpost_ref, wpre_ref, sc_ref, sh_ref, wr_ref,
                     x1_ref, hp_ref, lg_ref):
    y = y_ref[...]
    yn = y * lax.rsqrt(jnp.mean(y * y, axis=-1, keepdims=True) + RMS_EPS) * wpost_ref[...]
    x1 = x_ref[...] + gm_ref[...] * yn
    x1_ref[...] = x1
    h = x1 * lax.rsqrt(jnp.mean(x1 * x1, axis=-1, keepdims=True) + RMS_EPS) * wpre_ref[...]
    h = h * (1.0 + sc_ref[...]) + sh_ref[...]
    hhi, hlo = _split2(h)
    whi, wlo = _split2(wr_ref[...])
    lg_ref[...] = _dot(hhi, whi) + _dot(hhi, wlo) + _dot(hlo, whi)
    half = h.shape[1] // 2
    top = pltpu.bitcast(hhi[:, :half].astype(f32), jnp.uint32)
    bot = pltpu.bitcast(hhi[:, half:].astype(f32), jnp.uint32)
    hp_ref[...] = top | (bot >> 16)


def _post_mix(y, x, gm, wpost, wpre, sc, sh, wr_pad):
    s, d = x.shape
    tm = 256
    row = pl.BlockSpec((1, d), lambda i: (0, 0))
    tile = pl.BlockSpec((tm, d), lambda i: (i, 0))
    return pl.pallas_call(
        _post_mix_kernel,
        grid=(s // tm,),
        in_specs=[tile, tile, row, row, row, row, row, pl.BlockSpec((d, LANE), lambda i: (0, 0))],
        out_specs=[tile, pl.BlockSpec((tm, d // 2), lambda i: (i, 0)),
                   pl.BlockSpec((tm, LANE), lambda i: (i, 0))],
        out_shape=[jax.ShapeDtypeStruct((s, d), f32), jax.ShapeDtypeStruct((s, d // 2), jnp.uint32),
                   jax.ShapeDtypeStruct((s, LANE), f32)],
        compiler_params=_params(("parallel",), VMEM_LIMIT),
        name="post_mix",
    )(y, x, gm, wpost.reshape(1, d), wpre.reshape(1, d), sc, sh, wr_pad)


def _select_kernel(lg_ref, utri_ref, pos_ref, sel_ref, aff_ref, *, seq, cap):
    E = N_EXPERTS
    nb = seq // LANE
    lg = lg_ref[...]
    lane = lax.broadcasted_iota(jnp.int32, lg.shape, 1)
    lg = jnp.where(lane < E, lg, -jnp.inf)
    mx = jnp.max(lg, axis=-1, keepdims=True)
    ex = jnp.exp(lg - mx)
    aff = ex / jnp.sum(ex, axis=-1, keepdims=True)
    aff_t = aff.T[:E, :]
    aff_ref[...] = aff_t
    bits = pltpu.bitcast(aff_t, jnp.int32)

    def search(i, thr):
        cand = thr | (jnp.int32(1) << (30 - i))
        cnt = jnp.sum((bits >= cand).astype(jnp.int32), axis=-1, keepdims=True)
        return jnp.where(cnt >= cap, cand, thr)

    thr = lax.fori_loop(0, 31, search, jnp.zeros((E, 1), jnp.int32))
    gt = bits > thr
    eq = bits == thr
    need = cap - jnp.sum(gt.astype(jnp.int32), axis=-1, keepdims=True)
    utri = utri_ref[...]

    def cumsum_excl(mask_f):
        carry = jnp.zeros((E, 1), f32)
        outs = []
        for b in range(nb):
            blk = mask_f[:, b * LANE:(b + 1) * LANE]
            inc = _dot(blk.astype(bf16), utri)
            outs.append(inc - blk + carry)
            carry = carry + inc[:, LANE - 1:LANE]
        return jnp.concatenate(outs, axis=1)

    eq_f = eq.astype(f32)
    rank_eq = cumsum_excl(eq_f)
    sel = gt | (eq & (rank_eq < need.astype(f32)))
    sel_f = sel.astype(f32)
    pos = cumsum_excl(sel_f)
    pos_ref[...] = pos
    sel_ref[...] = sel_f


def _select(logits, cap):
    s = logits.shape[0]
    t = np.arange(LANE)
    utri = jnp.asarray((t[:, None] <= t[None, :]).astype(np.float32), bf16)
    full = lambda shape: pl.BlockSpec(shape, lambda: tuple(0 for _ in shape))
    rows = jax.ShapeDtypeStruct((N_EXPERTS, s), f32)
    return pl.pallas_call(
        functools.partial(_select_kernel, seq=s, cap=cap),
        in_specs=[full((s, LANE)), full((LANE, LANE))],
        out_specs=[full((N_EXPERTS, s))] * 3,
        out_shape=[rows, rows, rows],
        compiler_params=_params(None, VMEM_LIMIT),
        name="ec_select",
    )(logits, utri)


def _extract_plan(pos, sel, cap, group):
    incl_end = (pos + sel)[:, LANE - 1::LANE]
    before = jnp.concatenate([jnp.zeros_like(incl_end[:, :1]), incl_end[:, :-1]], axis=1)
    p0 = (jnp.arange(cap // group) * group).astype(f32)
    lo = jnp.sum(incl_end[:, None, :] <= p0[None, :, None], axis=-1)
    hi = jnp.sum(before[:, None, :] < (p0 + group)[None, :, None], axis=-1)
    return lo.astype(jnp.int32), hi.astype(jnp.int32)


def _extract_kernel(lo_ref, hi_ref, pos_ref, sel_ref, aff_ref, idx_ref, gate_ref, *, cap, group):
    e = pl.program_id(0)
    prow = lax.broadcasted_iota(jnp.int32, (group, LANE), 0).astype(f32)
    for c in range(cap // group):
        pcol = prow + float(c * group)
        lo = lo_ref[e, c]

        def body(b, carry):
            cnt, g = carry
            c0 = pl.multiple_of(b * LANE, LANE)
            selr = sel_ref[0, :, pl.ds(c0, LANE)]
            incl = pos_ref[0, :, pl.ds(c0, LANE)] + selr
            affr = aff_ref[0, :, pl.ds(c0, LANE)]
            cnt = cnt + jnp.where(incl <= pcol, 1.0, 0.0)
            g = g + jnp.where((selr > 0.0) & (incl == pcol + 1.0), affr, 0.0)
            return cnt, g

        zero = jnp.zeros((group, LANE), f32)
        cnt, g = lax.fori_loop(lo, hi_ref[e, c], body, (zero, zero))
        idx_col = jnp.sum(cnt, axis=-1, keepdims=True) + (lo * LANE).astype(f32)
        idx_ref[0, c * group:(c + 1) * group, :] = idx_col.astype(jnp.int32)
        gate_ref[0, c * group:(c + 1) * group, :] = jnp.sum(g, axis=-1, keepdims=True)


def _extract(pos, sel, aff, cap):
    s = pos.shape[1]
    group = min(LANE, cap)
    lo, hi = _extract_plan(pos, sel, cap, group)
    full = pl.BlockSpec((1, 1, s), lambda e, *_: (e, 0, 0))
    col = pl.BlockSpec((1, cap, 1), lambda e, *_: (e, 0, 0))
    grid_spec = pltpu.PrefetchScalarGridSpec(
        num_scalar_prefetch=2, grid=(N_EXPERTS,),
        in_specs=[full, full, full], out_specs=[col, col])
    return pl.pallas_call(
        functools.partial(_extract_kernel, cap=cap, group=group),
        grid_spec=grid_spec,
        out_shape=[jax.ShapeDtypeStruct((N_EXPERTS, cap, 1), jnp.int32),
                   jax.ShapeDtypeStruct((N_EXPERTS, cap, 1), f32)],
        compiler_params=_params(("parallel",)),
        name="ec_extract",
    )(lo, hi, *[a.reshape(N_EXPERTS, 1, s) for a in (pos, sel, aff)])


FF_TILE = 256
DOWN_TILE = 512


def _expert_kernel(idx_ref, hp_ref, wg_ref, wu_ref, wd_ref, gate_ref, y_ref,
                   xbuf_ref, xa_ref, xb_ref, hid_ref, sem, *, cap):
    e = pl.program_id(0)
    s = pl.program_id(1)
    n_ff = D_FF_EXPERT // FF_TILE
    half = D_MODEL // 2

    def gather(ex):
        def start(p, c):
            pltpu.make_async_copy(hp_ref.at[pl.ds(idx_ref[ex, p], 1), :],
                                  xbuf_ref.at[pl.ds(p, 1), :], sem).start()
            return c

        lax.fori_loop(0, cap, start, 0, unroll=8)

    @pl.when((s == 0) & (e == 0))
    def _():
        gather(0)

    @pl.when(s == 0)
    def _():
        pltpu.make_async_copy(hp_ref.at[pl.ds(0, cap), :], xbuf_ref, sem).wait()
        u = xbuf_ref[...]
        xa_ref[...] = pltpu.bitcast(u & jnp.uint32(0xFFFF0000), f32).astype(bf16)
        xb_ref[...] = pltpu.bitcast(u << 16, f32).astype(bf16)

    @pl.when((s == 0) & (e + 1 < N_EXPERTS))
    def _():
        gather(e + 1)

    @pl.when(s < n_ff)
    def _():
        wg = wg_ref[0].astype(bf16)
        wu = wu_ref[0].astype(bf16)
        xa = xa_ref[...]
        xb = xb_ref[...]
        hg = _dot(xa, wg[:half]) + _dot(xb, wg[half:])
        hu = _dot(xa, wu[:half]) + _dot(xb, wu[half:])
        c0 = pl.multiple_of(s * FF_TILE, FF_TILE)
        hid_ref[:, pl.ds(c0, FF_TILE)] = (_silu(hg) * hu).astype(bf16)

    @pl.when(s >= n_ff)
    def _():
        y = _dot(hid_ref[...], wd_ref[0].astype(bf16))
        y_ref[0] = (y * gate_ref[0]).astype(y_ref.dtype)


def _experts(idx, hp, w_gate, w_up, w_down, gate, cap):
    n_ff = D_FF_EXPERT // FF_TILE
    n_dn = D_MODEL // DOWN_TILE
    ffj = lambda s: jnp.minimum(s, n_ff - 1)
    dnj = lambda s: jnp.maximum(s - n_ff, 0)
    grid_spec = pltpu.PrefetchScalarGridSpec(
        num_scalar_prefetch=1,
        grid=(N_EXPERTS, n_ff + n_dn),
        in_specs=[pl.BlockSpec(memory_space=pl.ANY),
                  pl.BlockSpec((1, D_MODEL, FF_TILE), lambda e, s, idx: (e, 0, ffj(s))),
                  pl.BlockSpec((1, D_MODEL, FF_TILE), lambda e, s, idx: (e, 0, ffj(s))),
                  pl.BlockSpec((1, D_FF_EXPERT, DOWN_TILE), lambda e, s, idx: (e, 0, dnj(s))),
                  pl.BlockSpec((1, cap, 1), lambda e, s, idx: (e, 0, 0))],
        out_specs=pl.BlockSpec((1, cap, DOWN_TILE), lambda e, s, idx: (e, 0, dnj(s))),
        scratch_shapes=[pltpu.VMEM((cap, D_MODEL // 2), jnp.uint32),
                        pltpu.VMEM((cap, D_MODEL // 2), bf16),
                        pltpu.VMEM((cap, D_MODEL // 2), bf16),
                        pltpu.VMEM((cap, D_FF_EXPERT), bf16),
                        pltpu.SemaphoreType.DMA(())],
    )
    return pl.pallas_call(
        functools.partial(_expert_kernel, cap=cap),
        grid_spec=grid_spec,
        out_shape=jax.ShapeDtypeStruct((N_EXPERTS, cap, D_MODEL), bf16),
        compiler_params=_params(("arbitrary", "arbitrary"), VMEM_LIMIT),
        name="ec_experts",
    )(idx, hp, w_gate, w_up, w_down, gate)


COMBINE_ALIGN = 16


COMBINE_K = 256


def _combine_plan(pos, cap, tt):
    al = COMBINE_ALIGN
    first = pos[:, ::tt].astype(jnp.int32)
    nxt = jnp.concatenate([first[:, 1:], jnp.full((first.shape[0], 1), cap, jnp.int32)], axis=1)
    start = first // al * al
    pieces = jnp.where(nxt > first, (nxt - start + al - 1) // al, 0)
    base = (jnp.cumsum(pieces, axis=0) - pieces) * al
    total = jnp.sum(pieces, axis=0)
    chunks = (total * al + COMBINE_K - 1) // COMBINE_K
    return start, pieces, base, total, chunks


def _combine_kernel(start_ref, pieces_ref, base_ref, total_ref, chunks_ref,
                    y_ref, pos_ref, sel_ref, x1_ref, gf_ref, w_ref, o_ref,
                    stage_ref, acc_ref, sem, *, tt):
    i = pl.program_id(0)
    al = COMBINE_ALIGN

    @pl.when(i == 0)
    def _():
        stage_ref[...] = jnp.zeros_like(stage_ref)

    for e in range(N_EXPERTS):
        def issue(j, c):
            src = pl.multiple_of(start_ref[e, i] + j * al, al)
            dst = pl.multiple_of(base_ref[e, i] + j * al, al)
            pltpu.make_async_copy(y_ref.at[e, pl.ds(src, al), :], stage_ref.at[pl.ds(dst, al), :], sem).start()
            return c

        lax.fori_loop(0, pieces_ref[e, i], issue, 0)

    keys = []
    for e in range(N_EXPERTS):
        shift = (base_ref[e, i] - start_ref[e, i]).astype(f32)
        keys.append(jnp.where(sel_ref[e:e + 1, :] > 0.0, pos_ref[e:e + 1, :] + shift, -1.0))

    def wait(j, c):
        pltpu.make_async_copy(y_ref.at[0, pl.ds(0, al), :], stage_ref.at[pl.ds(0, al), :], sem).wait()
        return c

    lax.fori_loop(0, total_ref[i], wait, 0)
    acc_ref[...] = jnp.zeros_like(acc_ref)
    riota = lax.broadcasted_iota(jnp.int32, (COMBINE_K, tt), 0).astype(f32)

    def chunk(k, c):
        r0 = pl.multiple_of(k * COMBINE_K, COMBINE_K)
        rows = riota + r0.astype(f32)
        et = jnp.zeros((COMBINE_K, tt), f32)
        for e in range(N_EXPERTS):
            et = jnp.where(keys[e] == rows, 1.0, et)
        acc_ref[...] += lax.dot_general(et.astype(bf16), stage_ref[pl.ds(r0, COMBINE_K), :],
                                        (((0,), (0,)), ((), ())), preferred_element_type=f32)
        return c

    lax.fori_loop(0, chunks_ref[i], chunk, 0)
    y = acc_ref[...]
    yn = y * lax.rsqrt(jnp.mean(y * y, axis=-1, keepdims=True) + RMS_EPS) * w_ref[...]
    o_ref[...] = x1_ref[...] + gf_ref[...] * yn


def _combine(plan, yexp, pos, sel, x1, gf, w_post, tt):
    s, d = x1.shape
    stage_rows = N_EXPERTS * (tt + 2 * COMBINE_ALIGN)
    stage_rows = (stage_rows + COMBINE_K - 1) // COMBINE_K * COMBINE_K
    row = pl.BlockSpec((1, d), lambda i, *_: (0, 0))
    grid_spec = pltpu.PrefetchScalarGridSpec(
        num_scalar_prefetch=5,
        grid=(s // tt,),
        in_specs=[pl.BlockSpec(memory_space=pl.ANY),
                  pl.BlockSpec((N_EXPERTS, tt), lambda i, *_: (0, i)),
                  pl.BlockSpec((N_EXPERTS, tt), lambda i, *_: (0, i)),
                  pl.BlockSpec((tt, d), lambda i, *_: (i, 0)), row, row],
        out_specs=pl.BlockSpec((tt, d), lambda i, *_: (i, 0)),
        scratch_shapes=[pltpu.VMEM((stage_rows, d), bf16), pltpu.VMEM((tt, d), f32),
                        pltpu.SemaphoreType.DMA(())],
    )
    return pl.pallas_call(
        functools.partial(_combine_kernel, tt=tt),
        grid_spec=grid_spec,
        out_shape=jax.ShapeDtypeStruct((s, d), f32),
        compiler_params=_params(("arbitrary",), VMEM_LIMIT),
        name="ec_combine",
    )(*plan, yexp, pos, sel, x1, gf, w_post.reshape(1, d))


GATE_SUB = 256


def _gate_extras(proj, off, tm, tn):
    n_sub = tn // GATE_SUB
    return [(proj, (tm, GATE_SUB), functools.partial(
        lambda m, j, u: (m, off // GATE_SUB + j * n_sub + u), u=u)) for u in range(n_sub)]


def _merge_a(acc, *gate_refs):
    ga = jnp.concatenate([r[...] for r in gate_refs], axis=1).astype(f32)
    return _sigmoid(ga) * acc


def _merge_b(acc, part_ref, *gate_refs):
    gb = jnp.concatenate([r[...] for r in gate_refs], axis=1).astype(f32)
    return part_ref[...] + _sigmoid(gb) * acc


def _layer(x, mod, lb, norm_pre_mix, norm_post_mix, norm_pre_ffn, norm_post_ffn,
           w_in, conv_w, conv_b, dt_bias, a_log, d_skip, ssd_norm_w, hg_norm_w,
           w_ssd_out, w_hg_out, w_mix_out, w_router, w_gate, w_up, w_down):
    s, d = x.shape
    sh_m, sc_m, g_m, sh_f, sc_f, g_f = [mod[:, i * d:(i + 1) * d] for i in range(6)]

    h = _prenorm(x, norm_pre_mix, sc_m, sh_m)
    tm8, tm16 = s // 8, s // 16
    proj = _wsmm(h, w_in, tm=tm8, tn=1280, out_dtype=bf16, name="in_proj")

    xact = _conv(proj, conv_w, conv_b)
    tri_f, tri_b, sel_a, ea = _ssd_consts()
    cb = lambda a: jnp.asarray(a, bf16)
    prep = _ssd_prep(proj, dt_bias, a_log, (cb(tri_f), cb(tri_b), cb(sel_a)))
    y_f = _ssd(xact, *prep[:4], cb(ea), rev=False)
    dskip = jnp.repeat(d_skip, SSD_HEAD_DIM).reshape(1, SSD_D_INNER)
    y_ssd = _ssd(xact, *prep[4:], cb(ea), rev=True,
                 final_args=(y_f, proj, ssd_norm_w.reshape(1, SSD_D_INNER), dskip))

    hg_consts = [tuple((cb(tri), cb(tsm), jnp.asarray(msk))) for tri, tsm, msk in
                 (_hg_consts(False), _hg_consts(True))]
    scans = _hgrn(proj, lb, hg_consts[0], hg_consts[1])
    y_hg = _hg_out(scans, proj, hg_norm_w)

    part = _wsmm(y_ssd, w_ssd_out, tm=tm16, tn=512, out_dtype=f32,
                 extras=_gate_extras(proj, OFF_GA, tm16, 512),
                 epilogue=_merge_a, name="ssd_out_proj")
    merged = _wsmm(y_hg, w_hg_out, tm=tm8, tn=512, out_dtype=bf16,
                   extras=[(part, (tm8, 512), lambda m, j: (m, j))]
                   + _gate_extras(proj, OFF_GB, tm8, 512),
                   epilogue=_merge_b, name="hg_out_proj")
    y_mix = _wsmm(merged, w_mix_out, tm=tm8, tn=512, out_dtype=f32, name="mix_out_proj")

    wr_pad = jnp.pad(w_router, ((0, 0), (0, LANE - N_EXPERTS)))
    x1, hp, logits = _post_mix(y_mix, x, g_m, norm_post_mix, norm_pre_ffn, sc_f, sh_f, wr_pad)
    cap = 2 * s // N_EXPERTS
    pos, sel, aff = _select(logits, cap)
    idx, gate = _extract(pos, sel, aff, cap)
    yexp = _experts(idx.reshape(N_EXPERTS, cap), hp, w_gate, w_up, w_down, gate, cap)
    tt = min(128, cap // 2)
    plan = _combine_plan(pos, cap, tt)
    return _combine(plan, yexp, pos, sel, x1, g_f, norm_post_ffn, tt)


def kernel(x, c, w_ada, b_ada, norm_pre_mix, norm_post_mix, norm_pre_ffn, norm_post_ffn, w_in, conv_w, conv_b, dt_bias_fwd, dt_bias_bwd, a_log_fwd, a_log_bwd, d_skip, ssd_norm_w, hg_lower_bound, hg_norm_w, w_ssd_out, w_hg_out, w_mix_out, w_router, w_gate, w_up, w_down):
    depth = w_ada.shape[0]
    lower_bounds = jnp.cumsum(jax.nn.softmax(hg_lower_bound.astype(f32), axis=0), axis=0)
    outs = []
    for bi in range(x.shape[0]):
        xb = x[bi]
        cb_ = c[bi:bi + 1]
        for l in range(depth):
            mod = _ada(cb_, w_ada[l], b_ada[l])
            dt_bias = jnp.concatenate([dt_bias_fwd[l], dt_bias_bwd[l]]).reshape(1, -1)
            a_log = jnp.concatenate([a_log_fwd[l], a_log_bwd[l]]).reshape(1, -1)
            xb = _layer(xb, mod, lower_bounds[l].reshape(1, -1), norm_pre_mix[l], norm_post_mix[l],
                        norm_pre_ffn[l], norm_post_ffn[l], w_in[l], conv_w[l], conv_b[l],
                        dt_bias, a_log, d_skip[l], ssd_norm_w[l], hg_norm_w[l],
                        w_ssd_out[l], w_hg_out[l], w_mix_out[l], w_router[l],
                        w_gate[l], w_up[l], w_down[l])
        outs.append(xb)
    return jnp.stack(outs, axis=0)
```

```python
import functools

import numpy as np
import jax
import jax.numpy as jnp
from jax import lax
from jax.experimental import pallas as pl
from jax.experimental.pallas import tpu as pltpu

f32 = jnp.float32
bf16 = jnp.bfloat16

D_MODEL = 4096
SSD_D_INNER = 8192
SSD_HEADS = 128
SSD_HEAD_DIM = 64
SSD_D_STATE = 128
SSD_GROUPS = 8
SSD_HEADS_PER_GROUP = 16
SSD_CONV = 5
SSD_XBC = SSD_D_INNER + 2 * SSD_GROUPS * SSD_D_STATE
HG_HEADS = 32
HG_HEAD_DIM = 128
N_EXPERTS = 16
D_FF_EXPERT = 2048
RMS_EPS = 1e-6

OFF_Z = 0
OFF_XBC = OFF_Z + SSD_D_INNER
OFF_DT = OFF_XBC + SSD_XBC
OFF_Q = OFF_DT + 2 * SSD_HEADS
OFF_FF = OFF_Q + D_MODEL
OFF_FB = OFF_FF + D_MODEL
OFF_I = OFF_FB + D_MODEL
OFF_G = OFF_I + D_MODEL
OFF_GA = OFF_G + D_MODEL
OFF_GB = OFF_GA + D_MODEL
D_IN_PROJ = OFF_GB + D_MODEL

LANE = 128
CHUNK = 128
VMEM_LIMIT = 56 * 1024 * 1024


def _params(sem, vmem=None):
    return pltpu.CompilerParams(dimension_semantics=sem, vmem_limit_bytes=vmem)


def _sigmoid(x):
    return 1.0 / (1.0 + jnp.exp(-x))


def _silu(x):
    return x * _sigmoid(x)


def _softplus(x):
    return jnp.maximum(x, 0.0) + jnp.log(1.0 + jnp.exp(-jnp.abs(x)))


def _split2(x):
    hi = x.astype(bf16)
    lo = (x - hi.astype(f32)).astype(bf16)
    return hi, lo


def _split3(x):
    hi = x.astype(bf16)
    r = x - hi.astype(f32)
    mid = r.astype(bf16)
    lo = (r - mid.astype(f32)).astype(bf16)
    return hi, mid, lo


def _dot(a, b):
    return jnp.dot(a, b, preferred_element_type=f32)


def _dot_nt(a, b):
    return lax.dot_general(a, b, (((1,), (1,)), ((), ())), preferred_element_type=f32)


def _ada_kernel(c_ref, w_ref, b_ref, o_ref):
    c = c_ref[...]
    ca = jnp.broadcast_to(_silu(c), (8, c.shape[1]))
    chi, clo = _split2(ca)
    whi, wlo = _split2(w_ref[...])
    acc = _dot(chi, whi) + _dot(chi, wlo) + _dot(clo, whi)
    o_ref[...] = acc[0:1, :] + b_ref[...]


def _ada(c, w, b):
    d, n = w.shape
    tn = 512
    return pl.pallas_call(
        _ada_kernel,
        grid=(n // tn,),
        in_specs=[pl.BlockSpec((1, d), lambda j: (0, 0)),
                  pl.BlockSpec((d, tn), lambda j: (0, j)),
                  pl.BlockSpec((1, tn), lambda j: (0, j))],
        out_specs=pl.BlockSpec((1, tn), lambda j: (0, j)),
        out_shape=jax.ShapeDtypeStruct((1, n), f32),
        compiler_params=_params(("parallel",), VMEM_LIMIT),
        name="adaln",
    )(c, w, b.reshape(1, n))


def _prenorm_kernel(x_ref, w_ref, sc_ref, sh_ref, o_ref):
    x = x_ref[...]
    r = lax.rsqrt(jnp.mean(x * x, axis=-1, keepdims=True) + RMS_EPS)
    h = (x * r * w_ref[...]) * (1.0 + sc_ref[...]) + sh_ref[...]
    o_ref[...] = h.astype(o_ref.dtype)


def _prenorm(x, w, sc, sh):
    s, d = x.shape
    tm = 256
    row = pl.BlockSpec((1, d), lambda i: (0, 0))
    return pl.pallas_call(
        _prenorm_kernel,
        grid=(s // tm,),
        in_specs=[pl.BlockSpec((tm, d), lambda i: (i, 0)), row, row, row],
        out_specs=pl.BlockSpec((tm, d), lambda i: (i, 0)),
        out_shape=jax.ShapeDtypeStruct((s, d), bf16),
        compiler_params=_params(("parallel",)),
        name="prenorm",
    )(x, w.reshape(1, d), sc, sh)


def _wsmm_kernel(*refs, n_extra, nn, epilogue):
    a_ref, w_ref = refs[0], refs[1]
    extras = refs[2:2 + n_extra]
    o_ref = refs[2 + n_extra]
    wbuf_ref = refs[3 + n_extra]
    n = pl.program_id(0)
    m = pl.program_id(1)
    kc = w_ref.shape[0]

    @pl.when(n < nn)
    def _():
        r0 = pl.multiple_of(m * kc, kc)
        wbuf_ref[n % 2, pl.ds(r0, kc), :] = w_ref[...].astype(bf16)

    @pl.when(n == 0)
    def _():
        o_ref[...] = jnp.zeros_like(o_ref)

    @pl.when(n > 0)
    def _():
        acc = _dot(a_ref[...], wbuf_ref[(n - 1) % 2])
        o_ref[...] = epilogue(acc, *extras).astype(o_ref.dtype)


def _wsmm(a, w, *, tm, tn, out_dtype, extras=(), epilogue=None, name="matmul"):
    m, kdim = a.shape
    n = w.shape[1]
    nm, nn = m // tm, n // tn
    kc = kdim // nm
    if epilogue is None:
        epilogue = lambda acc: acc
    jn = lambda nidx: jnp.maximum(nidx - 1, 0)
    extra_arrays = [e[0] for e in extras]
    extra_specs = [pl.BlockSpec(e[1], functools.partial(lambda nidx, midx, f: f(midx, jn(nidx)), f=e[2]))
                   for e in extras]
    return pl.pallas_call(
        functools.partial(_wsmm_kernel, n_extra=len(extras), nn=nn, epilogue=epilogue),
        grid=(nn + 1, nm),
        in_specs=[pl.BlockSpec((tm, kdim), lambda nidx, midx: (jnp.where(nidx == 0, 0, midx), 0)),
                  pl.BlockSpec((kc, tn), lambda nidx, midx: (midx, jnp.minimum(nidx, nn - 1)))] + extra_specs,
        out_specs=pl.BlockSpec((tm, tn), lambda nidx, midx: (jnp.where(nidx == 0, 0, midx), jn(nidx))),
        out_shape=jax.ShapeDtypeStruct((m, n), out_dtype),
        scratch_shapes=[pltpu.VMEM((2, kdim, tn), bf16)],
        compiler_params=_params(("arbitrary", "arbitrary"), VMEM_LIMIT),
        name=name,
    )(a, w, *extra_arrays)


CONV_HALO = 16


def _conv_kernel(x_ref, w_ref, b_ref, o_ref, *, seq, tile):
    w = w_ref[...]
    b = b_ref[...]
    nt = seq // tile
    ext_rows = tile + 2 * CONV_HALO

    def body(i, carry):
        r0 = pl.multiple_of(i * tile, tile)
        cur = x_ref[pl.ds(r0, tile), :].astype(f32)
        p0 = pl.multiple_of(jnp.maximum(r0 - CONV_HALO, 0), CONV_HALO)
        n0 = pl.multiple_of(jnp.minimum(r0 + tile, seq - CONV_HALO), CONV_HALO)
        prev = jnp.where(i > 0, x_ref[pl.ds(p0, CONV_HALO), :].astype(f32), 0.0)
        nxt = jnp.where(i < nt - 1, x_ref[pl.ds(n0, CONV_HALO), :].astype(f32), 0.0)
        ext = jnp.concatenate([prev, cur, nxt], axis=0)
        acc = b + w[2:3, :] * cur
        for k in (0, 1, 3, 4):
            d = k - SSD_CONV // 2
            shifted = pltpu.roll(ext, (-d) % ext_rows, axis=0)[CONV_HALO:CONV_HALO + tile, :]
            acc = acc + w[k:k + 1, :] * shifted
        o_ref[pl.ds(r0, tile), :] = _silu(acc).astype(o_ref.dtype)
        return carry

    lax.fori_loop(0, nt, body, 0)


def _conv(proj, conv_w, conv_b):
    s = proj.shape[0]
    cw = 256
    off = OFF_XBC // cw
    tile = min(512, s)
    return pl.pallas_call(
        functools.partial(_conv_kernel, seq=s, tile=tile),
        grid=(SSD_XBC // cw,),
        in_specs=[pl.BlockSpec((s, cw), lambda j: (0, off + j)),
                  pl.BlockSpec((SSD_CONV, cw), lambda j: (0, j)),
                  pl.BlockSpec((1, cw), lambda j: (0, j))],
        out_specs=pl.BlockSpec((s, cw), lambda j: (0, j)),
        out_shape=jax.ShapeDtypeStruct((s, SSD_XBC), bf16),
        compiler_params=_params(("parallel",), VMEM_LIMIT),
        name="conv_silu",
    )(proj, conv_w, conv_b.reshape(1, SSD_XBC))


def _ssd_consts():
    L = CHUNK
    t = np.arange(L)
    tri_f = (t[None, :] <= t[:, None]).astype(np.float32)
    tri_b = (t[None, :] >= t[:, None]).astype(np.float32)
    h = np.arange(SSD_HEADS)
    g, r = h // SSD_HEADS_PER_GROUP, h % SSD_HEADS_PER_GROUP
    sel_a = np.zeros((3 * SSD_HEADS, SSD_GROUPS * LANE), np.float32)
    for j in range(3):
        sel_a[j * SSD_HEADS + h, g * LANE + j * SSD_HEADS_PER_GROUP + r] = 1.0
    sel_w = np.zeros((6 * SSD_HEADS, SSD_GROUPS * LANE), np.float32)
    for j in range(6):
        sel_w[j * SSD_HEADS + h, g * LANE + j * SSD_HEADS_PER_GROUP + r] = 1.0
    ea = np.zeros((LANE, SSD_HEADS_PER_GROUP * L), np.float32)
    for j in range(3):
        for rr in range(SSD_HEADS_PER_GROUP):
            ea[j * SSD_HEADS_PER_GROUP + rr, rr * L:(rr + 1) * L] = 1.0
    gw = SSD_HEADS_PER_GROUP * SSD_HEAD_DIM
    ew = np.zeros((LANE, 3 * gw), np.float32)
    for j in range(6):
        for rr in range(SSD_HEADS_PER_GROUP):
            ew[j * SSD_HEADS_PER_GROUP + rr,
               (j // 2) * gw + rr * SSD_HEAD_DIM:(j // 2) * gw + (rr + 1) * SSD_HEAD_DIM] = 1.0
    return tri_f, tri_b, sel_a, sel_w, ea, ew


def _ssd_prep_kernel(dt_ref, bias_ref, alog_ref, trif_ref, trib_ref, sela_ref, selw_ref,
                     paf_ref, pwf_ref, arf_ref, pab_ref, pwb_ref, arb_ref):
    raw = dt_ref[...].astype(f32)
    L = raw.shape[0]
    outs = ((paf_ref, pwf_ref, arf_ref, trif_ref, L - 1), (pab_ref, pwb_ref, arb_ref, trib_ref, 0))
    for d, (pa_ref, pw_ref, ar_ref, tri_ref, tot_row) in enumerate(outs):
        sl = slice(d * SSD_HEADS, (d + 1) * SSD_HEADS)
        dt = _softplus(raw[:, sl] + bias_ref[:, sl])
        adt = dt * (-jnp.exp(alog_ref[:, sl]))
        h3 = jnp.concatenate(_split3(adt), axis=1)
        a3 = _dot(tri_ref[...], h3)
        a = (a3[:, :SSD_HEADS] + a3[:, SSD_HEADS:2 * SSD_HEADS]) + a3[:, 2 * SSD_HEADS:]
        atot = a[tot_row:tot_row + 1, :]
        wgt = jnp.exp(atot - a) * dt
        ea = jnp.exp(a)
        ar_ref[...] = a.T
        pa_ref[...] = _dot(jnp.concatenate(_split3(a), axis=1), sela_ref[...]).astype(bf16)
        cols = _split2(wgt) + _split2(ea) + _split2(dt)
        pw_ref[...] = _dot(jnp.concatenate(cols, axis=1), selw_ref[...]).astype(bf16)


def _ssd_prep(proj, dt_bias, a_log, consts):
    s = proj.shape[0]
    L = CHUNK
    tri_f, tri_b, sel_a, sel_w = consts
    gl = SSD_GROUPS * LANE
    const = lambda shape: pl.BlockSpec(shape, lambda c: (0, 0))
    packed = pl.BlockSpec((L, gl), lambda c: (c, 0))
    rowf = pl.BlockSpec((SSD_HEADS, L), lambda c: (0, c))
    return pl.pallas_call(
        _ssd_prep_kernel,
        grid=(s // L,),
        in_specs=[pl.BlockSpec((L, 2 * SSD_HEADS), lambda c: (c, OFF_DT // (2 * SSD_HEADS))),
                  const((1, 2 * SSD_HEADS)), const((1, 2 * SSD_HEADS)),
                  const((L, L)), const((L, L)),
                  const(sel_a.shape), const(sel_w.shape)],
        out_specs=[packed, packed, rowf, packed, packed, rowf],
        out_shape=[jax.ShapeDtypeStruct((s, gl), bf16), jax.ShapeDtypeStruct((s, gl), bf16),
                   jax.ShapeDtypeStruct((SSD_HEADS, s), f32)] * 2,
        compiler_params=_params(("parallel",)),
        name="ssd_prep",
    )(proj, dt_bias, a_log, tri_f, tri_b, sel_a, sel_w)


def _ssd_kernel(*refs, rev, final):
    if final:
        (x_ref, b_ref, c_ref, pa_ref, pw_ref, ar_ref, ea_ref, ew_ref,
         yprev_ref, z_ref, nw_ref, dsk_ref, o_ref, st_ref) = refs
    else:
        x_ref, b_ref, c_ref, pa_ref, pw_ref, ar_ref, ea_ref, ew_ref, o_ref, st_ref = refs
    L = CHUNK
    gw = SSD_HEADS_PER_GROUP * SSD_HEAD_DIM
    npair = SSD_HEADS_PER_GROUP // 2

    @pl.when(pl.program_id(1) == 0)
    def _():
        st_ref[...] = jnp.zeros_like(st_ref)

    x = x_ref[...]
    bm = b_ref[...]
    cm = c_ref[...]
    aexp = _dot(pa_ref[...], ea_ref[...])
    wed = _dot(pw_ref[...], ew_ref[...])
    wexp, eexp, dtexp = wed[:, :gw], wed[:, gw:2 * gw], wed[:, 2 * gw:]
    xf = x.astype(f32)
    xdt = (xf * dtexp).astype(bf16)
    xw = (xf * wexp).astype(bf16)
    tot_row = 0 if rev else L - 1
    dec_tot = eexp[tot_row:tot_row + 1, :]
    bt = bm.astype(f32).T.astype(bf16)
    cb = _dot(cm, bt)
    ti = lax.broadcasted_iota(jnp.int32, (L, L), 0)
    si = lax.broadcasted_iota(jnp.int32, (L, L), 1)
    keep = (si >= ti) if rev else (si <= ti)
    cbm = jnp.where(keep, cb, 0.0)
    lane = lax.broadcasted_iota(jnp.int32, (L, LANE), 1)
    first = lane < SSD_HEAD_DIM
    ys = []
    for j in range(npair):
        ds_ = []
        for r in (2 * j, 2 * j + 1):
            diff = aexp[:, r * L:(r + 1) * L] - ar_ref[r:r + 1, :]
            ds_.append((jnp.exp(jnp.minimum(diff, 0.0)) * cbm).astype(bf16))
        mp = jnp.concatenate(ds_, axis=1)
        xp = xdt[:, j * LANE:(j + 1) * LANE]
        zero = jnp.zeros_like(xp)
        xbd = jnp.concatenate([jnp.where(first, xp, zero), jnp.where(first, zero, xp)], axis=0)
        st = st_ref[j]
        y = _dot(mp, xbd) + _dot(cm, st.astype(bf16)) * eexp[:, j * LANE:(j + 1) * LANE]
        ys.append(y)
        st_ref[j] = st * dec_tot[:, j * LANE:(j + 1) * LANE] + _dot(bt, xw[:, j * LANE:(j + 1) * LANE])
    y = jnp.concatenate(ys, axis=1)
    if final:
        y = y + yprev_ref[...].astype(f32) + dsk_ref[...] * xf
        y = y * _silu(z_ref[...].astype(f32))
        y = y * lax.rsqrt(jnp.mean(y * y, axis=-1, keepdims=True) + RMS_EPS) * nw_ref[...]
    o_ref[...] = y.astype(o_ref.dtype)


def _ssd(xact, pa, pw, arow, ea, ew, *, rev, final_args=None):
    s = xact.shape[0]
    L = CHUNK
    nc = s // L
    gw = SSD_HEADS_PER_GROUP * SSD_HEAD_DIM
    cc = (lambda c: nc - 1 - c) if rev else (lambda c: c)
    nxg = SSD_D_INNER // LANE
    in_specs = [pl.BlockSpec((L, gw), lambda g, c: (cc(c), g)),
                pl.BlockSpec((L, LANE), lambda g, c: (cc(c), nxg + g)),
                pl.BlockSpec((L, LANE), lambda g, c: (cc(c), nxg + SSD_GROUPS + g)),
                pl.BlockSpec((L, LANE), lambda g, c: (cc(c), g)),
                pl.BlockSpec((L, LANE), lambda g, c: (cc(c), g)),
                pl.BlockSpec((SSD_HEADS_PER_GROUP, L), lambda g, c: (g, cc(c))),
                pl.BlockSpec(ea.shape, lambda g, c: (0, 0)),
                pl.BlockSpec(ew.shape, lambda g, c: (0, 0))]
    args = [xact, xact, xact, pa, pw, arow, ea, ew]
    final = final_args is not None
    if final:
        yprev, proj, norm_w, dskip = final_args
        in_specs += [pl.BlockSpec((L, gw), lambda g, c: (cc(c), g)),
                     pl.BlockSpec((L, gw), lambda g, c: (cc(c), OFF_Z // gw + g)),
                     pl.BlockSpec((1, gw), lambda g, c: (0, g)),
                     pl.BlockSpec((1, gw), lambda g, c: (0, g))]
        args += [yprev, proj, norm_w, dskip]
    return pl.pallas_call(
        functools.partial(_ssd_kernel, rev=rev, final=final),
        grid=(SSD_GROUPS, nc),
        in_specs=in_specs,
        out_specs=pl.BlockSpec((L, gw), lambda g, c: (cc(c), g)),
        out_shape=jax.ShapeDtypeStruct((s, SSD_D_INNER), bf16),
        scratch_shapes=[pltpu.VMEM((SSD_HEADS_PER_GROUP // 2, SSD_D_STATE, LANE), f32)],
        compiler_params=_params(("parallel", "arbitrary")),
        name="ssd_bwd" if rev else "ssd_fwd",
    )(*args)


HG_LEVELS = 8
HG_MM_LEVELS = 2
HG_PAIR = 2
HG_BLOCKS = 2
HG_SUB = 4


def _hg_level_ref_row(lv, blk, rev):
    b = 1 << (lv - 1)
    return blk * 2 * b + (b if rev else b - 1)


def _hg_consts(rev):
    C = CHUNK
    t = np.arange(C)
    if rev:
        tri = (t[None, :] >= t[:, None]).astype(np.float32)
    else:
        tri = (t[None, :] <= t[:, None]).astype(np.float32)
    tmats = []
    masks = [(t[:, None] == t[None, :]).astype(np.float32)]
    for lv in range(1, HG_LEVELS):
        b = 1 << (lv - 1)
        blk = t // (2 * b)
        upper = (t % (2 * b)) >= b
        if rev:
            mask = (blk[:, None] == blk[None, :]) & (~upper)[:, None] & upper[None, :]
        else:
            mask = (blk[:, None] == blk[None, :]) & upper[:, None] & (~upper)[None, :]
        if lv <= HG_MM_LEVELS:
            tmats.append(tri - tri[_hg_level_ref_row(lv, blk, rev)])
        masks.append(mask.astype(np.float32))
    return tri, np.concatenate(tmats, axis=0), np.stack(masks, axis=0)


def _hg_chain(qb, fb, vb, lb, tri_ref, tsm_ref, m_ref, g_ref, st_ref, rev):
    C = CHUNK
    qf = qb.astype(f32)
    fr = fb.astype(f32)
    sg = _sigmoid(fr)
    lf = jnp.log2(lb + (1.0 - lb) * sg)
    kk = (1.0 - lb) * (1.0 - sg)
    kb = kk.astype(bf16)
    h3 = _split3(lf)
    g3 = _dot(tri_ref[...], jnp.concatenate(h3, axis=1))
    g = (g3[:, :LANE] + g3[:, LANE:2 * LANE]) + g3[:, 2 * LANE:]
    e2 = _dot(tsm_ref[...], jnp.concatenate(h3[:2], axis=1))
    e_small = e2[:, :LANE] + e2[:, LANE:]
    g_ref[...] = g
    sc = _dot_nt(qb, kb) * m_ref[0]
    for lv in range(1, HG_LEVELS):
        if lv <= HG_MM_LEVELS:
            e = e_small[(lv - 1) * C:lv * C, :]
        else:
            b = 1 << (lv - 1)
            rows = []
            for blk in range(C // (2 * b)):
                m = _hg_level_ref_row(lv, blk, rev)
                rows.append(jnp.broadcast_to(g_ref[m:m + 1, :], (2 * b, LANE)))
            e = g - jnp.concatenate(rows, axis=0)
        xd = jnp.exp2(-jnp.abs(e)).astype(bf16)
        sc = sc + _dot_nt(qb * xd, kb * xd) * m_ref[lv]
    o = _dot(sc.astype(bf16), vb)
    st = st_ref[...]
    o = o + _dot_nt((qf * jnp.exp2(g)).astype(bf16), st.astype(bf16))
    tot_row = 0 if rev else C - 1
    gtot = g[tot_row:tot_row + 1, :]
    kdec = (kk * jnp.exp2(gtot - g)).astype(bf16)
    vt = vb.astype(f32).T.astype(bf16)
    st_ref[...] = st * jnp.exp2(gtot) + _dot(vt, kdec)
    return o


def _hg_kernel(*refs):
    n_in = 7 * HG_BLOCKS
    data = refs[:n_in]
    trif_ref, tsmf_ref, mf_ref, trib_ref, tsmb_ref, mb_ref = refs[n_in:n_in + 6]
    outs = refs[n_in + 6:n_in + 6 + 2 * HG_BLOCKS]
    st_ref, g_ref = refs[-2:]

    @pl.when(pl.program_id(1) == 0)
    def _():
        st_ref[...] = jnp.zeros_like(st_ref)

    C = CHUNK
    for sub in range(HG_SUB):
        k = 0
        for u in range(HG_BLOCKS):
            qf_ref, ff_ref, vf_ref, qb_ref, fb_ref, vb_ref, lb_ref = data[7 * u:7 * u + 7]
            of_ref, ob_ref = outs[2 * u:2 * u + 2]
            dirs = ((qf_ref, ff_ref, vf_ref, trif_ref, tsmf_ref, mf_ref, of_ref, False),
                    (qb_ref, fb_ref, vb_ref, trib_ref, tsmb_ref, mb_ref, ob_ref, True))
            for q_ref, f_ref, v_ref, tri_ref, tsm_ref, m_ref, o_ref, rev in dirs:
                r0 = (HG_SUB - 1 - sub if rev else sub) * C
                rows = slice(r0, r0 + C)
                for hh in range(HG_PAIR):
                    sl = slice(hh * LANE, (hh + 1) * LANE)
                    o = _hg_chain(q_ref[rows, sl], f_ref[rows, sl], v_ref[rows, sl], lb_ref[:, sl],
                                  tri_ref, tsm_ref, m_ref, g_ref.at[k], st_ref.at[k], rev)
                    o_ref[rows, sl] = o.astype(o_ref.dtype)
                    k += 1


def _hgrn(proj, lb, consts_f, consts_b):
    s = proj.shape[0]
    C = HG_SUB * CHUNK
    nc = s // C
    w = HG_PAIR * LANE
    nb = HG_BLOCKS
    const = lambda a: pl.BlockSpec(a.shape, lambda h, c: tuple(0 for _ in a.shape))
    consts = list(consts_f) + list(consts_b)
    in_specs, args, out_specs = [], [], []
    for u in range(nb):
        fwd = lambda off, u=u: pl.BlockSpec((C, w), lambda h, c: (c, off // w + nb * h + u))
        bwd = lambda off, u=u: pl.BlockSpec((C, w), lambda h, c: (nc - 1 - c, off // w + nb * h + u))
        in_specs += [fwd(OFF_Q), fwd(OFF_FF), fwd(OFF_I), bwd(OFF_Q), bwd(OFF_FB), bwd(OFF_I),
                     pl.BlockSpec((1, w), lambda h, c, u=u: (0, nb * h + u))]
        args += [proj] * 6 + [lb]
        out_specs += [pl.BlockSpec((C, w), lambda h, c: (c, h)),
                      pl.BlockSpec((C, w), lambda h, c: (nc - 1 - c, h))]
    n_chain = 2 * nb * HG_PAIR
    outs = pl.pallas_call(
        _hg_kernel,
        grid=(HG_HEADS // (HG_PAIR * nb), nc),
        in_specs=in_specs + [const(a) for a in consts],
        out_specs=out_specs,
        out_shape=[jax.ShapeDtypeStruct((s, D_MODEL // nb), bf16)] * (2 * nb),
        scratch_shapes=[pltpu.VMEM((n_chain, HG_HEAD_DIM, HG_HEAD_DIM), f32),
                        pltpu.VMEM((n_chain, CHUNK, LANE), f32)],
        compiler_params=_params(("parallel", "arbitrary")),
        name="hgrn",
    )(*args, *consts)
    return outs


HG_GATE_COLS = 256


def _hg_out_kernel(*refs):
    nb = HG_BLOCKS
    scan_refs, w_ref = refs[:2 * nb], refs[2 * nb]
    g_refs, y_ref = refs[2 * nb + 1:-1], refs[-1]
    w = HG_PAIR * LANE
    parts = []
    for u in range(nb):
        parts.append(scan_refs[2 * u][...].astype(f32) + scan_refs[2 * u + 1][...].astype(f32))
    cols = [parts[blk % nb][:, (blk // nb) * w:(blk // nb + 1) * w] for blk in range(D_MODEL // w)]
    o = jnp.concatenate(cols, axis=1)
    g = jnp.concatenate([r[...] for r in g_refs], axis=1).astype(f32)
    rs = lax.rsqrt(jnp.mean(o * o, axis=-1, keepdims=True) + RMS_EPS)
    y_ref[...] = (o * rs * w_ref[...] * _silu(g)).astype(y_ref.dtype)


def _hg_out(scans, proj, norm_w):
    s = proj.shape[0]
    d = D_MODEL
    tm = 256
    cw = HG_GATE_COLS
    tile = pl.BlockSpec((tm, d), lambda i: (i, 0))
    part = pl.BlockSpec((tm, d // HG_BLOCKS), lambda i: (i, 0))
    gates = [pl.BlockSpec((tm, cw), functools.partial(lambda i, u: (i, OFF_G // cw + u), u=u))
             for u in range(d // cw)]
    return pl.pallas_call(
        _hg_out_kernel,
        grid=(s // tm,),
        in_specs=[part] * len(scans) + [pl.BlockSpec((1, d), lambda i: (0, 0))] + gates,
        out_specs=tile,
        out_shape=jax.ShapeDtypeStruct((s, d), bf16),
        compiler_params=_params(("parallel",)),
        name="hgrn_out",
    )(*scans, norm_w.reshape(1, d), *([proj] * len(gates)))


def _post_mix_kernel(y_ref, x_ref, gm_ref, wpost_ref, wpre_ref, sc_ref, sh_ref, wr_ref,
                     x1_ref, hp_ref, lg_ref):
    y = y_ref[...]
    yn = y * lax.rsqrt(jnp.mean(y * y, axis=-1, keepdims=True) + RMS_EPS) * wpost_ref[...]
    x1 = x_ref[...] + gm_ref[...] * yn
    x1_ref[...] = x1
    h = x1 * lax.rsqrt(jnp.mean(x1 * x1, axis=-1, keepdims=True) + RMS_EPS) * wpre_ref[...]
    h = h * (1.0 + sc_ref[...]) + sh_ref[...]
    hhi, hlo = _split2(h)
    whi, wlo = _split2(wr_ref[...])
    lg_ref[...] = _dot(hhi, whi) + _dot(hhi, wlo) + _dot(hlo, whi)
    half = h.shape[1] // 2
    top = pltpu.bitcast(hhi[:, :half].astype(f32), jnp.uint32)
    bot = pltpu.bitcast(hhi[:, half:].astype(f32), jnp.uint32)
    hp_ref[...] = top | (bot >> 16)


def _post_mix(y, x, gm, wpost, wpre, sc, sh, wr_pad):
    s, d = x.shape
    tm = 256
    row = pl.BlockSpec((1, d), lambda i: (0, 0))
    tile = pl.BlockSpec((tm, d), lambda i: (i, 0))
    return pl.pallas_call(
        _post_mix_kernel,
        grid=(s // tm,),
        in_specs=[tile, tile, row, row, row, row, row, pl.BlockSpec((d, LANE), lambda i: (0, 0))],
        out_specs=[tile, pl.BlockSpec((tm, d // 2), lambda i: (i, 0)),
                   pl.BlockSpec((tm, LANE), lambda i: (i, 0))],
        out_shape=[jax.ShapeDtypeStruct((s, d), f32), jax.ShapeDtypeStruct((s, d // 2), jnp.uint32),
                   jax.ShapeDtypeStruct((s, LANE), f32)],
        compiler_params=_params(("parallel",), VMEM_LIMIT),
        name="post_mix",
    )(y, x, gm, wpost.reshape(1, d), wpre.reshape(1, d), sc, sh, wr_pad)


def _select_kernel(lg_ref, utri_ref, pos_ref, sel_ref, aff_ref, *, seq, cap):
    E = N_EXPERTS
    nb = seq // LANE
    lg = lg_ref[...]
    lane = lax.broadcasted_iota(jnp.int32, lg.shape, 1)
    lg = jnp.where(lane < E, lg, -jnp.inf)
    mx = jnp.max(lg, axis=-1, keepdims=True)
    ex = jnp.exp(lg - mx)
    aff = ex / jnp.sum(ex, axis=-1, keepdims=True)
    aff_t = aff.T[:E, :]
    aff_ref[...] = aff_t
    bits = pltpu.bitcast(aff_t, jnp.int32)

    def search(i, thr):
        cand = thr | (jnp.int32(1) << (30 - i))
        cnt = jnp.sum((bits >= cand).astype(jnp.int32), axis=-1, keepdims=True)
        return jnp.where(cnt >= cap, cand, thr)

    thr = lax.fori_loop(0, 31, search, jnp.zeros((E, 1), jnp.int32))
    gt = bits > thr
    eq = bits == thr
    need = cap - jnp.sum(gt.astype(jnp.int32), axis=-1, keepdims=True)
    utri = utri_ref[...]

    def cumsum_excl(mask_f):
        carry = jnp.zeros((E, 1), f32)
        outs = []
        for b in range(nb):
            blk = mask_f[:, b * LANE:(b + 1) * LANE]
            inc = _dot(blk.astype(bf16), utri)
            outs.append(inc - blk + carry)
            carry = carry + inc[:, LANE - 1:LANE]
        return jnp.concatenate(outs, axis=1)

    eq_f = eq.astype(f32)
    rank_eq = cumsum_excl(eq_f)
    sel = gt | (eq & (rank_eq < need.astype(f32)))
    sel_f = sel.astype(f32)
    pos = cumsum_excl(sel_f)
    pos_ref[...] = pos
    sel_ref[...] = sel_f


def _select(logits, cap):
    s = logits.shape[0]
    t = np.arange(LANE)
    utri = jnp.asarray((t[:, None] <= t[None, :]).astype(np.float32), bf16)
    full = lambda shape: pl.BlockSpec(shape, lambda: tuple(0 for _ in shape))
    rows = jax.ShapeDtypeStruct((N_EXPERTS, s), f32)
    return pl.pallas_call(
        functools.partial(_select_kernel, seq=s, cap=cap),
        in_specs=[full((s, LANE)), full((LANE, LANE))],
        out_specs=[full((N_EXPERTS, s))] * 3,
        out_shape=[rows, rows, rows],
        compiler_params=_params(None, VMEM_LIMIT),
        name="ec_select",
    )(logits, utri)


def _extract_plan(pos, sel, cap, group):
    incl_end = (pos + sel)[:, LANE - 1::LANE]
    before = jnp.concatenate([jnp.zeros_like(incl_end[:, :1]), incl_end[:, :-1]], axis=1)
    p0 = (jnp.arange(cap // group) * group).astype(f32)
    lo = jnp.sum(incl_end[:, None, :] <= p0[None, :, None], axis=-1)
    hi = jnp.sum(before[:, None, :] < (p0 + group)[None, :, None], axis=-1)
    return lo.astype(jnp.int32), hi.astype(jnp.int32)


def _extract_kernel(lo_ref, hi_ref, pos_ref, sel_ref, aff_ref, idx_ref, gate_ref, *, cap, group):
    e = pl.program_id(0)
    prow = lax.broadcasted_iota(jnp.int32, (group, LANE), 0).astype(f32)
    for c in range(cap // group):
        pcol = prow + float(c * group)
        lo = lo_ref[e, c]

        def body(b, carry):
            cnt, g = carry
            c0 = pl.multiple_of(b * LANE, LANE)
            selr = sel_ref[0, :, pl.ds(c0, LANE)]
            incl = pos_ref[0, :, pl.ds(c0, LANE)] + selr
            affr = aff_ref[0, :, pl.ds(c0, LANE)]
            cnt = cnt + jnp.where(incl <= pcol, 1.0, 0.0)
            g = g + jnp.where((selr > 0.0) & (incl == pcol + 1.0), affr, 0.0)
            return cnt, g

        zero = jnp.zeros((group, LANE), f32)
        cnt, g = lax.fori_loop(lo, hi_ref[e, c], body, (zero, zero))
        idx_col = jnp.sum(cnt, axis=-1, keepdims=True) + (lo * LANE).astype(f32)
        idx_ref[0, c * group:(c + 1) * group, :] = idx_col.astype(jnp.int32)
        gate_ref[0, c * group:(c + 1) * group, :] = jnp.sum(g, axis=-1, keepdims=True)


def _extract(pos, sel, aff, cap):
    s = pos.shape[1]
    group = min(LANE, cap)
    lo, hi = _extract_plan(pos, sel, cap, group)
    full = pl.BlockSpec((1, 1, s), lambda e, *_: (e, 0, 0))
    col = pl.BlockSpec((1, cap, 1), lambda e, *_: (e, 0, 0))
    grid_spec = pltpu.PrefetchScalarGridSpec(
        num_scalar_prefetch=2, grid=(N_EXPERTS,),
        in_specs=[full, full, full], out_specs=[col, col])
    return pl.pallas_call(
        functools.partial(_extract_kernel, cap=cap, group=group),
        grid_spec=grid_spec,
        out_shape=[jax.ShapeDtypeStruct((N_EXPERTS, cap, 1), jnp.int32),
                   jax.ShapeDtypeStruct((N_EXPERTS, cap, 1), f32)],
        compiler_params=_params(("parallel",)),
        name="ec_extract",
    )(lo, hi, *[a.reshape(N_EXPERTS, 1, s) for a in (pos, sel, aff)])


FF_TILE = 256
DOWN_TILE = 512


def _expert_kernel(idx_ref, hp_ref, wg_ref, wu_ref, wd_ref, gate_ref, y_ref,
                   xbuf_ref, xa_ref, xb_ref, hid_ref, sem, *, cap):
    e = pl.program_id(0)
    s = pl.program_id(1)
    n_ff = D_FF_EXPERT // FF_TILE
    half = D_MODEL // 2

    def gather(ex):
        def start(p, c):
            pltpu.make_async_copy(hp_ref.at[pl.ds(idx_ref[ex, p], 1), :],
                                  xbuf_ref.at[pl.ds(p, 1), :], sem).start()
            return c

        lax.fori_loop(0, cap, start, 0, unroll=8)

    @pl.when((s == 0) & (e == 0))
    def _():
        gather(0)

    @pl.when(s == 0)
    def _():
        pltpu.make_async_copy(hp_ref.at[pl.ds(0, cap), :], xbuf_ref, sem).wait()
        u = xbuf_ref[...]
        xa_ref[...] = pltpu.bitcast(u & jnp.uint32(0xFFFF0000), f32).astype(bf16)
        xb_ref[...] = pltpu.bitcast(u << 16, f32).astype(bf16)

    @pl.when((s == 0) & (e + 1 < N_EXPERTS))
    def _():
        gather(e + 1)

    @pl.when(s < n_ff)
    def _():
        wg = wg_ref[0].astype(bf16)
        wu = wu_ref[0].astype(bf16)
        xa = xa_ref[...]
        xb = xb_ref[...]
        hg = _dot(xa, wg[:half]) + _dot(xb, wg[half:])
        hu = _dot(xa, wu[:half]) + _dot(xb, wu[half:])
        c0 = pl.multiple_of(s * FF_TILE, FF_TILE)
        hid_ref[:, pl.ds(c0, FF_TILE)] = (_silu(hg) * hu).astype(bf16)

    @pl.when(s >= n_ff)
    def _():
        y = _dot(hid_ref[...], wd_ref[0].astype(bf16))
        y_ref[0] = (y * gate_ref[0]).astype(y_ref.dtype)


def _experts(idx, hp, w_gate, w_up, w_down, gate, cap):
    n_ff = D_FF_EXPERT // FF_TILE
    n_dn = D_MODEL // DOWN_TILE
    ffj = lambda s: jnp.minimum(s, n_ff - 1)
    dnj = lambda s: jnp.maximum(s - n_ff, 0)
    grid_spec = pltpu.PrefetchScalarGridSpec(
        num_scalar_prefetch=1,
        grid=(N_EXPERTS, n_ff + n_dn),
        in_specs=[pl.BlockSpec(memory_space=pl.ANY),
                  pl.BlockSpec((1, D_MODEL, FF_TILE), lambda e, s, idx: (e, 0, ffj(s))),
                  pl.BlockSpec((1, D_MODEL, FF_TILE), lambda e, s, idx: (e, 0, ffj(s))),
                  pl.BlockSpec((1, D_FF_EXPERT, DOWN_TILE), lambda e, s, idx: (e, 0, dnj(s))),
                  pl.BlockSpec((1, cap, 1), lambda e, s, idx: (e, 0, 0))],
        out_specs=pl.BlockSpec((1, cap, DOWN_TILE), lambda e, s, idx: (e, 0, dnj(s))),
        scratch_shapes=[pltpu.VMEM((cap, D_MODEL // 2), jnp.uint32),
                        pltpu.VMEM((cap, D_MODEL // 2), bf16),
                        pltpu.VMEM((cap, D_MODEL // 2), bf16),
                        pltpu.VMEM((cap, D_FF_EXPERT), bf16),
                        pltpu.SemaphoreType.DMA(())],
    )
    return pl.pallas_call(
        functools.partial(_expert_kernel, cap=cap),
        grid_spec=grid_spec,
        out_shape=jax.ShapeDtypeStruct((N_EXPERTS, cap, D_MODEL), bf16),
        compiler_params=_params(("arbitrary", "arbitrary"), VMEM_LIMIT),
        name="ec_experts",
    )(idx, hp, w_gate, w_up, w_down, gate)


COMBINE_ALIGN = 16


COMBINE_K = 256


def _combine_plan(pos, cap, tt):
    al = COMBINE_ALIGN
    first = pos[:, ::tt].astype(jnp.int32)
    nxt = jnp.concatenate([first[:, 1:], jnp.full((first.shape[0], 1), cap, jnp.int32)], axis=1)
    start = first // al * al
    pieces = jnp.where(nxt > first, (nxt - start + al - 1) // al, 0)
    base = (jnp.cumsum(pieces, axis=0) - pieces) * al
    total = jnp.sum(pieces, axis=0)
    chunks = (total * al + COMBINE_K - 1) // COMBINE_K
    return start, pieces, base, total, chunks


def _combine_kernel(start_ref, pieces_ref, base_ref, total_ref, chunks_ref,
                    y_ref, pos_ref, sel_ref, x1_ref, gf_ref, w_ref, o_ref,
                    stage_ref, acc_ref, sem, *, tt):
    i = pl.program_id(0)
    al = COMBINE_ALIGN

    @pl.when(i == 0)
    def _():
        stage_ref[...] = jnp.zeros_like(stage_ref)

    for e in range(N_EXPERTS):
        def issue(j, c):
            src = pl.multiple_of(start_ref[e, i] + j * al, al)
            dst = pl.multiple_of(base_ref[e, i] + j * al, al)
            pltpu.make_async_copy(y_ref.at[e, pl.ds(src, al), :], stage_ref.at[pl.ds(dst, al), :], sem).start()
            return c

        lax.fori_loop(0, pieces_ref[e, i], issue, 0)

    keys = []
    for e in range(N_EXPERTS):
        shift = (base_ref[e, i] - start_ref[e, i]).astype(f32)
        keys.append(jnp.where(sel_ref[e:e + 1, :] > 0.0, pos_ref[e:e + 1, :] + shift, -1.0))

    def wait(j, c):
        pltpu.make_async_copy(y_ref.at[0, pl.ds(0, al), :], stage_ref.at[pl.ds(0, al), :], sem).wait()
        return c

    lax.fori_loop(0, total_ref[i], wait, 0)
    acc_ref[...] = jnp.zeros_like(acc_ref)
    riota = lax.broadcasted_iota(jnp.int32, (COMBINE_K, tt), 0).astype(f32)

    def chunk(k, c):
        r0 = pl.multiple_of(k * COMBINE_K, COMBINE_K)
        rows = riota + r0.astype(f32)
        et = jnp.zeros((COMBINE_K, tt), f32)
        for e in range(N_EXPERTS):
            et = jnp.where(keys[e] == rows, 1.0, et)
        acc_ref[...] += lax.dot_general(et.astype(bf16), stage_ref[pl.ds(r0, COMBINE_K), :],
                                        (((0,), (0,)), ((), ())), preferred_element_type=f32)
        return c

    lax.fori_loop(0, chunks_ref[i], chunk, 0)
    y = acc_ref[...]
    yn = y * lax.rsqrt(jnp.mean(y * y, axis=-1, keepdims=True) + RMS_EPS) * w_ref[...]
    o_ref[...] = x1_ref[...] + gf_ref[...] * yn


def _combine(plan, yexp, pos, sel, x1, gf, w_post, tt):
    s, d = x1.shape
    stage_rows = N_EXPERTS * (tt + 2 * COMBINE_ALIGN)
    stage_rows = (stage_rows + COMBINE_K - 1) // COMBINE_K * COMBINE_K
    row = pl.BlockSpec((1, d), lambda i, *_: (0, 0))
    grid_spec = pltpu.PrefetchScalarGridSpec(
        num_scalar_prefetch=5,
        grid=(s // tt,),
        in_specs=[pl.BlockSpec(memory_space=pl.ANY),
                  pl.BlockSpec((N_EXPERTS, tt), lambda i, *_: (0, i)),
                  pl.BlockSpec((N_EXPERTS, tt), lambda i, *_: (0, i)),
                  pl.BlockSpec((tt, d), lambda i, *_: (i, 0)), row, row],
        out_specs=pl.BlockSpec((tt, d), lambda i, *_: (i, 0)),
        scratch_shapes=[pltpu.VMEM((stage_rows, d), bf16), pltpu.VMEM((tt, d), f32),
                        pltpu.SemaphoreType.DMA(())],
    )
    return pl.pallas_call(
        functools.partial(_combine_kernel, tt=tt),
        grid_spec=grid_spec,
        out_shape=jax.ShapeDtypeStruct((s, d), f32),
        compiler_params=_params(("arbitrary",), VMEM_LIMIT),
        name="ec_combine",
    )(*plan, yexp, pos, sel, x1, gf, w_post.reshape(1, d))


GATE_SUB = 256


def _gate_extras(proj, off, tm, tn):
    n_sub = tn // GATE_SUB
    return [(proj, (tm, GATE_SUB), functools.partial(
        lambda m, j, u: (m, off // GATE_SUB + j * n_sub + u), u=u)) for u in range(n_sub)]


def _merge_a(acc, *gate_refs):
    ga = jnp.concatenate([r[...] for r in gate_refs], axis=1).astype(f32)
    return _sigmoid(ga) * acc


def _merge_b(acc, part_ref, *gate_refs):
    gb = jnp.concatenate([r[...] for r in gate_refs], axis=1).astype(f32)
    return part_ref[...] + _sigmoid(gb) * acc


def _layer(x, mod, lb, norm_pre_mix, norm_post_mix, norm_pre_ffn, norm_post_ffn,
           w_in, conv_w, conv_b, dt_bias, a_log, d_skip, ssd_norm_w, hg_norm_w,
           w_ssd_out, w_hg_out, w_mix_out, w_router, w_gate, w_up, w_down):
    s, d = x.shape
    sh_m, sc_m, g_m, sh_f, sc_f, g_f = [mod[:, i * d:(i + 1) * d] for i in range(6)]

    h = _prenorm(x, norm_pre_mix, sc_m, sh_m)
    tm8, tm16 = s // 8, s // 16
    proj = _wsmm(h, w_in, tm=tm8, tn=1280, out_dtype=bf16, name="in_proj")

    xact = _conv(proj, conv_w, conv_b)
    tri_f, tri_b, sel_a, sel_w, ea, ew = _ssd_consts()
    cb = lambda a: jnp.asarray(a, bf16)
    paf, pwf, arf, pab, pwb, arb = _ssd_prep(proj, dt_bias, a_log,
                                             (cb(tri_f), cb(tri_b), cb(sel_a), cb(sel_w)))
    y_f = _ssd(xact, paf, pwf, arf, cb(ea), cb(ew), rev=False)
    dskip = jnp.repeat(d_skip, SSD_HEAD_DIM).reshape(1, SSD_D_INNER)
    y_ssd = _ssd(xact, pab, pwb, arb, cb(ea), cb(ew), rev=True,
                 final_args=(y_f, proj, ssd_norm_w.reshape(1, SSD_D_INNER), dskip))

    hg_consts = [tuple((cb(tri), cb(tsm), jnp.asarray(msk))) for tri, tsm, msk in
                 (_hg_consts(False), _hg_consts(True))]
    scans = _hgrn(proj, lb, hg_consts[0], hg_consts[1])
    y_hg = _hg_out(scans, proj, hg_norm_w)

    part = _wsmm(y_ssd, w_ssd_out, tm=tm16, tn=512, out_dtype=f32,
                 extras=_gate_extras(proj, OFF_GA, tm16, 512),
                 epilogue=_merge_a, name="ssd_out_proj")
    merged = _wsmm(y_hg, w_hg_out, tm=tm8, tn=512, out_dtype=bf16,
                   extras=[(part, (tm8, 512), lambda m, j: (m, j))]
                   + _gate_extras(proj, OFF_GB, tm8, 512),
                   epilogue=_merge_b, name="hg_out_proj")
    y_mix = _wsmm(merged, w_mix_out, tm=tm8, tn=512, out_dtype=f32, name="mix_out_proj")

    wr_pad = jnp.pad(w_router, ((0, 0), (0, LANE - N_EXPERTS)))
    x1, hp, logits = _post_mix(y_mix, x, g_m, norm_post_mix, norm_pre_ffn, sc_f, sh_f, wr_pad)
    cap = 2 * s // N_EXPERTS
    pos, sel, aff = _select(logits, cap)
    idx, gate = _extract(pos, sel, aff, cap)
    yexp = _experts(idx.reshape(N_EXPERTS, cap), hp, w_gate, w_up, w_down, gate, cap)
    tt = min(128, cap // 2)
    plan = _combine_plan(pos, cap, tt)
    return _combine(plan, yexp, pos, sel, x1, g_f, norm_post_ffn, tt)


def kernel(x, c, w_ada, b_ada, norm_pre_mix, norm_post_mix, norm_pre_ffn, norm_post_ffn, w_in, conv_w, conv_b, dt_bias_fwd, dt_bias_bwd, a_log_fwd, a_log_bwd, d_skip, ssd_norm_w, hg_lower_bound, hg_norm_w, w_ssd_out, w_hg_out, w_mix_out, w_router, w_gate, w_up, w_down):
    depth = w_ada.shape[0]
    lower_bounds = jnp.cumsum(jax.nn.softmax(hg_lower_bound.astype(f32), axis=0), axis=0)
    outs = []
    for bi in range(x.shape[0]):
        xb = x[bi]
        cb_ = c[bi:bi + 1]
        for l in range(depth):
            mod = _ada(cb_, w_ada[l], b_ada[l])
            dt_bias = jnp.concatenate([dt_bias_fwd[l], dt_bias_bwd[l]]).reshape(1, -1)
            a_log = jnp.concatenate([a_log_fwd[l], a_log_bwd[l]]).reshape(1, -1)
            xb = _layer(xb, mod, lower_bounds[l].reshape(1, -1), norm_pre_mix[l], norm_post_mix[l],
                        norm_pre_ffn[l], norm_post_ffn[l], w_in[l], conv_w[l], conv_b[l],
                        dt_bias, a_log, d_skip[l], ssd_norm_w[l], hg_norm_w[l],
                        w_ssd_out[l], w_hg_out[l], w_mix_out[l], w_router[l],
                        w_gate[l], w_up[l], w_down[l])
        outs.append(xb)
    return jnp.stack(outs, axis=0)
```

```python
import functools

import numpy as np
import jax
import jax.numpy as jnp
from jax import lax
from jax.experimental import pallas as pl
from jax.experimental.pallas import tpu as pltpu

f32 = jnp.float32
bf16 = jnp.bfloat16

D_MODEL = 4096
SSD_D_INNER = 8192
SSD_HEADS = 128
SSD_HEAD_DIM = 64
SSD_D_STATE = 128
SSD_GROUPS = 8
SSD_HEADS_PER_GROUP = 16
SSD_CONV = 5
SSD_XBC = SSD_D_INNER + 2 * SSD_GROUPS * SSD_D_STATE
HG_HEADS = 32
HG_HEAD_DIM = 128
N_EXPERTS = 16
D_FF_EXPERT = 2048
RMS_EPS = 1e-6

OFF_Z = 0
OFF_XBC = OFF_Z + SSD_D_INNER
OFF_DT = OFF_XBC + SSD_XBC
OFF_Q = OFF_DT + 2 * SSD_HEADS
OFF_FF = OFF_Q + D_MODEL
OFF_FB = OFF_FF + D_MODEL
OFF_I = OFF_FB + D_MODEL
OFF_G = OFF_I + D_MODEL
OFF_GA = OFF_G + D_MODEL
OFF_GB = OFF_GA + D_MODEL
D_IN_PROJ = OFF_GB + D_MODEL

LANE = 128
CHUNK = 128
VMEM_LIMIT = 56 * 1024 * 1024


def _params(sem, vmem=None):
    return pltpu.CompilerParams(dimension_semantics=sem, vmem_limit_bytes=vmem)


def _sigmoid(x):
    return 1.0 / (1.0 + jnp.exp(-x))


def _silu(x):
    return x * _sigmoid(x)


def _softplus(x):
    return jnp.maximum(x, 0.0) + jnp.log(1.0 + jnp.exp(-jnp.abs(x)))


def _split2(x):
    hi = x.astype(bf16)
    lo = (x - hi.astype(f32)).astype(bf16)
    return hi, lo


def _split3(x):
    hi = x.astype(bf16)
    r = x - hi.astype(f32)
    mid = r.astype(bf16)
    lo = (r - mid.astype(f32)).astype(bf16)
    return hi, mid, lo


def _dot(a, b):
    return jnp.dot(a, b, preferred_element_type=f32)


def _dot_nt(a, b):
    return lax.dot_general(a, b, (((1,), (1,)), ((), ())), preferred_element_type=f32)


def _ada_kernel(c_ref, w_ref, b_ref, o_ref):
    c = c_ref[...]
    ca = jnp.broadcast_to(_silu(c), (8, c.shape[1]))
    chi, clo = _split2(ca)
    whi, wlo = _split2(w_ref[...])
    acc = _dot(chi, whi) + _dot(chi, wlo) + _dot(clo, whi)
    o_ref[...] = acc[0:1, :] + b_ref[...]


def _ada(c, w, b):
    d, n = w.shape
    tn = 512
    return pl.pallas_call(
        _ada_kernel,
        grid=(n // tn,),
        in_specs=[pl.BlockSpec((1, d), lambda j: (0, 0)),
                  pl.BlockSpec((d, tn), lambda j: (0, j)),
                  pl.BlockSpec((1, tn), lambda j: (0, j))],
        out_specs=pl.BlockSpec((1, tn), lambda j: (0, j)),
        out_shape=jax.ShapeDtypeStruct((1, n), f32),
        compiler_params=_params(("parallel",), VMEM_LIMIT),
        name="adaln",
    )(c, w, b.reshape(1, n))


def _prenorm_kernel(x_ref, w_ref, sc_ref, sh_ref, o_ref):
    x = x_ref[...]
    r = lax.rsqrt(jnp.mean(x * x, axis=-1, keepdims=True) + RMS_EPS)
    h = (x * r * w_ref[...]) * (1.0 + sc_ref[...]) + sh_ref[...]
    o_ref[...] = h.astype(o_ref.dtype)


def _prenorm(x, w, sc, sh):
    s, d = x.shape
    tm = 256
    row = pl.BlockSpec((1, d), lambda i: (0, 0))
    return pl.pallas_call(
        _prenorm_kernel,
        grid=(s // tm,),
        in_specs=[pl.BlockSpec((tm, d), lambda i: (i, 0)), row, row, row],
        out_specs=pl.BlockSpec((tm, d), lambda i: (i, 0)),
        out_shape=jax.ShapeDtypeStruct((s, d), bf16),
        compiler_params=_params(("parallel",)),
        name="prenorm",
    )(x, w.reshape(1, d), sc, sh)


def _wsmm_kernel(*refs, n_extra, nn, epilogue):
    a_ref, w_ref = refs[0], refs[1]
    extras = refs[2:2 + n_extra]
    o_ref = refs[2 + n_extra]
    wbuf_ref = refs[3 + n_extra]
    n = pl.program_id(0)
    m = pl.program_id(1)
    kc = w_ref.shape[0]

    @pl.when(n < nn)
    def _():
        r0 = pl.multiple_of(m * kc, kc)
        wbuf_ref[n % 2, pl.ds(r0, kc), :] = w_ref[...].astype(bf16)

    @pl.when(n == 0)
    def _():
        o_ref[...] = jnp.zeros_like(o_ref)

    @pl.when(n > 0)
    def _():
        acc = _dot(a_ref[...], wbuf_ref[(n - 1) % 2])
        o_ref[...] = epilogue(acc, *extras).astype(o_ref.dtype)


def _wsmm(a, w, *, tm, tn, out_dtype, extras=(), epilogue=None, name="matmul"):
    m, kdim = a.shape
    n = w.shape[1]
    nm, nn = m // tm, n // tn
    kc = kdim // nm
    if epilogue is None:
        epilogue = lambda acc: acc
    jn = lambda nidx: jnp.maximum(nidx - 1, 0)
    extra_arrays = [e[0] for e in extras]
    extra_specs = [pl.BlockSpec(e[1], functools.partial(lambda nidx, midx, f: f(midx, jn(nidx)), f=e[2]))
                   for e in extras]
    return pl.pallas_call(
        functools.partial(_wsmm_kernel, n_extra=len(extras), nn=nn, epilogue=epilogue),
        grid=(nn + 1, nm),
        in_specs=[pl.BlockSpec((tm, kdim), lambda nidx, midx: (jnp.where(nidx == 0, 0, midx), 0)),
                  pl.BlockSpec((kc, tn), lambda nidx, midx: (midx, jnp.minimum(nidx, nn - 1)))] + extra_specs,
        out_specs=pl.BlockSpec((tm, tn), lambda nidx, midx: (jnp.where(nidx == 0, 0, midx), jn(nidx))),
        out_shape=jax.ShapeDtypeStruct((m, n), out_dtype),
        scratch_shapes=[pltpu.VMEM((2, kdim, tn), bf16)],
        compiler_params=_params(("arbitrary", "arbitrary"), VMEM_LIMIT),
        name=name,
    )(a, w, *extra_arrays)


CONV_HALO = 16


def _conv_kernel(x_ref, w_ref, b_ref, o_ref, *, seq, tile):
    w = w_ref[...]
    b = b_ref[...]
    nt = seq // tile
    ext_rows = tile + 2 * CONV_HALO

    def body(i, carry):
        r0 = pl.multiple_of(i * tile, tile)
        cur = x_ref[pl.ds(r0, tile), :].astype(f32)
        p0 = pl.multiple_of(jnp.maximum(r0 - CONV_HALO, 0), CONV_HALO)
        n0 = pl.multiple_of(jnp.minimum(r0 + tile, seq - CONV_HALO), CONV_HALO)
        prev = jnp.where(i > 0, x_ref[pl.ds(p0, CONV_HALO), :].astype(f32), 0.0)
        nxt = jnp.where(i < nt - 1, x_ref[pl.ds(n0, CONV_HALO), :].astype(f32), 0.0)
        ext = jnp.concatenate([prev, cur, nxt], axis=0)
        acc = b + w[2:3, :] * cur
        for k in (0, 1, 3, 4):
            d = k - SSD_CONV // 2
            shifted = pltpu.roll(ext, (-d) % ext_rows, axis=0)[CONV_HALO:CONV_HALO + tile, :]
            acc = acc + w[k:k + 1, :] * shifted
        o_ref[pl.ds(r0, tile), :] = _silu(acc).astype(o_ref.dtype)
        return carry

    lax.fori_loop(0, nt, body, 0)


def _conv(proj, conv_w, conv_b):
    s = proj.shape[0]
    cw = 256
    off = OFF_XBC // cw
    tile = min(512, s)
    return pl.pallas_call(
        functools.partial(_conv_kernel, seq=s, tile=tile),
        grid=(SSD_XBC // cw,),
        in_specs=[pl.BlockSpec((s, cw), lambda j: (0, off + j)),
                  pl.BlockSpec((SSD_CONV, cw), lambda j: (0, j)),
                  pl.BlockSpec((1, cw), lambda j: (0, j))],
        out_specs=pl.BlockSpec((s, cw), lambda j: (0, j)),
        out_shape=jax.ShapeDtypeStruct((s, SSD_XBC), bf16),
        compiler_params=_params(("parallel",), VMEM_LIMIT),
        name="conv_silu",
    )(proj, conv_w, conv_b.reshape(1, SSD_XBC))


def _ssd_consts():
    L = CHUNK
    t = np.arange(L)
    tri_f = (t[None, :] <= t[:, None]).astype(np.float32)
    tri_b = (t[None, :] >= t[:, None]).astype(np.float32)
    h = np.arange(SSD_HEADS)
    g, r = h // SSD_HEADS_PER_GROUP, h % SSD_HEADS_PER_GROUP
    sel_a = np.zeros((3 * SSD_HEADS, SSD_GROUPS * LANE), np.float32)
    for j in range(3):
        sel_a[j * SSD_HEADS + h, g * LANE + j * SSD_HEADS_PER_GROUP + r] = 1.0
    sel_w = np.zeros((6 * SSD_HEADS, SSD_GROUPS * LANE), np.float32)
    for j in range(6):
        sel_w[j * SSD_HEADS + h, g * LANE + j * SSD_HEADS_PER_GROUP + r] = 1.0
    ea = np.zeros((LANE, SSD_HEADS_PER_GROUP * L), np.float32)
    for j in range(3):
        for rr in range(SSD_HEADS_PER_GROUP):
            ea[j * SSD_HEADS_PER_GROUP + rr, rr * L:(rr + 1) * L] = 1.0
    gw = SSD_HEADS_PER_GROUP * SSD_HEAD_DIM
    ew = np.zeros((LANE, 3 * gw), np.float32)
    for j in range(6):
        for rr in range(SSD_HEADS_PER_GROUP):
            ew[j * SSD_HEADS_PER_GROUP + rr,
               (j // 2) * gw + rr * SSD_HEAD_DIM:(j // 2) * gw + (rr + 1) * SSD_HEAD_DIM] = 1.0
    return tri_f, tri_b, sel_a, sel_w, ea, ew


def _ssd_prep_kernel(dt_ref, bias_ref, alog_ref, trif_ref, trib_ref, sela_ref, selw_ref,
                     paf_ref, pwf_ref, arf_ref, pab_ref, pwb_ref, arb_ref):
    raw = dt_ref[...].astype(f32)
    L = raw.shape[0]
    outs = ((paf_ref, pwf_ref, arf_ref, trif_ref, L - 1), (pab_ref, pwb_ref, arb_ref, trib_ref, 0))
    for d, (pa_ref, pw_ref, ar_ref, tri_ref, tot_row) in enumerate(outs):
        sl = slice(d * SSD_HEADS, (d + 1) * SSD_HEADS)
        dt = _softplus(raw[:, sl] + bias_ref[:, sl])
        adt = dt * (-jnp.exp(alog_ref[:, sl]))
        h3 = jnp.concatenate(_split3(adt), axis=1)
        a3 = _dot(tri_ref[...], h3)
        a = (a3[:, :SSD_HEADS] + a3[:, SSD_HEADS:2 * SSD_HEADS]) + a3[:, 2 * SSD_HEADS:]
        atot = a[tot_row:tot_row + 1, :]
        wgt = jnp.exp(atot - a) * dt
        ea = jnp.exp(a)
        ar_ref[...] = a.T
        pa_ref[...] = _dot(jnp.concatenate(_split3(a), axis=1), sela_ref[...]).astype(bf16)
        cols = _split2(wgt) + _split2(ea) + _split2(dt)
        pw_ref[...] = _dot(jnp.concatenate(cols, axis=1), selw_ref[...]).astype(bf16)


def _ssd_prep(proj, dt_bias, a_log, consts):
    s = proj.shape[0]
    L = CHUNK
    tri_f, tri_b, sel_a, sel_w = consts
    gl = SSD_GROUPS * LANE
    const = lambda shape: pl.BlockSpec(shape, lambda c: (0, 0))
    packed = pl.BlockSpec((L, gl), lambda c: (c, 0))
    rowf = pl.BlockSpec((SSD_HEADS, L), lambda c: (0, c))
    return pl.pallas_call(
        _ssd_prep_kernel,
        grid=(s // L,),
        in_specs=[pl.BlockSpec((L, 2 * SSD_HEADS), lambda c: (c, OFF_DT // (2 * SSD_HEADS))),
                  const((1, 2 * SSD_HEADS)), const((1, 2 * SSD_HEADS)),
                  const((L, L)), const((L, L)),
                  const(sel_a.shape), const(sel_w.shape)],
        out_specs=[packed, packed, rowf, packed, packed, rowf],
        out_shape=[jax.ShapeDtypeStruct((s, gl), bf16), jax.ShapeDtypeStruct((s, gl), bf16),
                   jax.ShapeDtypeStruct((SSD_HEADS, s), f32)] * 2,
        compiler_params=_params(("parallel",)),
        name="ssd_prep",
    )(proj, dt_bias, a_log, tri_f, tri_b, sel_a, sel_w)


def _ssd_kernel(*refs, rev, final):
    if final:
        (x_ref, b_ref, c_ref, pa_ref, pw_ref, ar_ref, ea_ref, ew_ref,
         yprev_ref, z_ref, nw_ref, dsk_ref, o_ref, st_ref) = refs
    else:
        x_ref, b_ref, c_ref, pa_ref, pw_ref, ar_ref, ea_ref, ew_ref, o_ref, st_ref = refs
    L = CHUNK
    gw = SSD_HEADS_PER_GROUP * SSD_HEAD_DIM
    npair = SSD_HEADS_PER_GROUP // 2

    @pl.when(pl.program_id(1) == 0)
    def _():
        st_ref[...] = jnp.zeros_like(st_ref)

    x = x_ref[...]
    bm = b_ref[...]
    cm = c_ref[...]
    aexp = _dot(pa_ref[...], ea_ref[...])
    wed = _dot(pw_ref[...], ew_ref[...])
    wexp, eexp, dtexp = wed[:, :gw], wed[:, gw:2 * gw], wed[:, 2 * gw:]
    xf = x.astype(f32)
    xdt = (xf * dtexp).astype(bf16)
    xw = (xf * wexp).astype(bf16)
    tot_row = 0 if rev else L - 1
    dec_tot = eexp[tot_row:tot_row + 1, :]
    bt = bm.astype(f32).T.astype(bf16)
    cb = _dot(cm, bt)
    ti = lax.broadcasted_iota(jnp.int32, (L, L), 0)
    si = lax.broadcasted_iota(jnp.int32, (L, L), 1)
    keep = (si >= ti) if rev else (si <= ti)
    cbm = jnp.where(keep, cb, 0.0)
    lane = lax.broadcasted_iota(jnp.int32, (L, LANE), 1)
    first = lane < SSD_HEAD_DIM
    ys = []
    for j in range(npair):
        ds_ = []
        for r in (2 * j, 2 * j + 1):
            diff = aexp[:, r * L:(r + 1) * L] - ar_ref[r:r + 1, :]
            ds_.append((jnp.exp(jnp.minimum(diff, 0.0)) * cbm).astype(bf16))
        mp = jnp.concatenate(ds_, axis=1)
        xp = xdt[:, j * LANE:(j + 1) * LANE]
        zero = jnp.zeros_like(xp)
        xbd = jnp.concatenate([jnp.where(first, xp, zero), jnp.where(first, zero, xp)], axis=0)
        st = st_ref[j]
        y = _dot(mp, xbd) + _dot(cm, st.astype(bf16)) * eexp[:, j * LANE:(j + 1) * LANE]
        ys.append(y)
        st_ref[j] = st * dec_tot[:, j * LANE:(j + 1) * LANE] + _dot(bt, xw[:, j * LANE:(j + 1) * LANE])
    y = jnp.concatenate(ys, axis=1)
    if final:
        y = y + yprev_ref[...].astype(f32) + dsk_ref[...] * xf
        y = y * _silu(z_ref[...].astype(f32))
        y = y * lax.rsqrt(jnp.mean(y * y, axis=-1, keepdims=True) + RMS_EPS) * nw_ref[...]
    o_ref[...] = y.astype(o_ref.dtype)


def _ssd(xact, pa, pw, arow, ea, ew, *, rev, final_args=None):
    s = xact.shape[0]
    L = CHUNK
    nc = s // L
    gw = SSD_HEADS_PER_GROUP * SSD_HEAD_DIM
    cc = (lambda c: nc - 1 - c) if rev else (lambda c: c)
    nxg = SSD_D_INNER // LANE
    in_specs = [pl.BlockSpec((L, gw), lambda g, c: (cc(c), g)),
                pl.BlockSpec((L, LANE), lambda g, c: (cc(c), nxg + g)),
                pl.BlockSpec((L, LANE), lambda g, c: (cc(c), nxg + SSD_GROUPS + g)),
                pl.BlockSpec((L, LANE), lambda g, c: (cc(c), g)),
                pl.BlockSpec((L, LANE), lambda g, c: (cc(c), g)),
                pl.BlockSpec((SSD_HEADS_PER_GROUP, L), lambda g, c: (g, cc(c))),
                pl.BlockSpec(ea.shape, lambda g, c: (0, 0)),
                pl.BlockSpec(ew.shape, lambda g, c: (0, 0))]
    args = [xact, xact, xact, pa, pw, arow, ea, ew]
    final = final_args is not None
    if final:
        yprev, proj, norm_w, dskip = final_args
        in_specs += [pl.BlockSpec((L, gw), lambda g, c: (cc(c), g)),
                     pl.BlockSpec((L, gw), lambda g, c: (cc(c), OFF_Z // gw + g)),
                     pl.BlockSpec((1, gw), lambda g, c: (0, g)),
                     pl.BlockSpec((1, gw), lambda g, c: (0, g))]
        args += [yprev, proj, norm_w, dskip]
    return pl.pallas_call(
        functools.partial(_ssd_kernel, rev=rev, final=final),
        grid=(SSD_GROUPS, nc),
        in_specs=in_specs,
        out_specs=pl.BlockSpec((L, gw), lambda g, c: (cc(c), g)),
        out_shape=jax.ShapeDtypeStruct((s, SSD_D_INNER), bf16),
        scratch_shapes=[pltpu.VMEM((SSD_HEADS_PER_GROUP // 2, SSD_D_STATE, LANE), f32)],
        compiler_params=_params(("parallel", "arbitrary")),
        name="ssd_bwd" if rev else "ssd_fwd",
    )(*args)


HG_LEVELS = 8
HG_MM_LEVELS = 2
HG_PAIR = 2
HG_BLOCKS = 2
HG_SUB = 8


def _hg_level_ref_row(lv, blk, rev):
    b = 1 << (lv - 1)
    return blk * 2 * b + (b if rev else b - 1)


def _hg_consts(rev):
    C = CHUNK
    t = np.arange(C)
    if rev:
        tri = (t[None, :] >= t[:, None]).astype(np.float32)
    else:
        tri = (t[None, :] <= t[:, None]).astype(np.float32)
    tmats = []
    masks = [(t[:, None] == t[None, :]).astype(np.float32)]
    for lv in range(1, HG_LEVELS):
        b = 1 << (lv - 1)
        blk = t // (2 * b)
        upper = (t % (2 * b)) >= b
        if rev:
            mask = (blk[:, None] == blk[None, :]) & (~upper)[:, None] & upper[None, :]
        else:
            mask = (blk[:, None] == blk[None, :]) & upper[:, None] & (~upper)[None, :]
        if lv <= HG_MM_LEVELS:
            tmats.append(tri - tri[_hg_level_ref_row(lv, blk, rev)])
        masks.append(mask.astype(np.float32))
    return tri, np.concatenate(tmats, axis=0), np.stack(masks, axis=0)


def _hg_chain(qb, fb, vb, lb, tri_ref, tsm_ref, m_ref, g_ref, st_ref, rev):
    C = CHUNK
    qf = qb.astype(f32)
    fr = fb.astype(f32)
    sg = _sigmoid(fr)
    lf = jnp.log2(lb + (1.0 - lb) * sg)
    kk = (1.0 - lb) * (1.0 - sg)
    kb = kk.astype(bf16)
    h3 = _split3(lf)
    g3 = _dot(tri_ref[...], jnp.concatenate(h3, axis=1))
    g = (g3[:, :LANE] + g3[:, LANE:2 * LANE]) + g3[:, 2 * LANE:]
    e2 = _dot(tsm_ref[...], jnp.concatenate(h3[:2], axis=1))
    e_small = e2[:, :LANE] + e2[:, LANE:]
    g_ref[...] = g
    sc = _dot_nt(qb, kb) * m_ref[0]
    for lv in range(1, HG_LEVELS):
        if lv <= HG_MM_LEVELS:
            e = e_small[(lv - 1) * C:lv * C, :]
        else:
            b = 1 << (lv - 1)
            rows = []
            for blk in range(C // (2 * b)):
                m = _hg_level_ref_row(lv, blk, rev)
                rows.append(jnp.broadcast_to(g_ref[m:m + 1, :], (2 * b, LANE)))
            e = g - jnp.concatenate(rows, axis=0)
        xd = jnp.exp2(-jnp.abs(e)).astype(bf16)
        sc = sc + _dot_nt(qb * xd, kb * xd) * m_ref[lv]
    o = _dot(sc.astype(bf16), vb)
    st = st_ref[...]
    o = o + _dot_nt((qf * jnp.exp2(g)).astype(bf16), st.astype(bf16))
    tot_row = 0 if rev else C - 1
    gtot = g[tot_row:tot_row + 1, :]
    kdec = (kk * jnp.exp2(gtot - g)).astype(bf16)
    vt = vb.astype(f32).T.astype(bf16)
    st_ref[...] = st * jnp.exp2(gtot) + _dot(vt, kdec)
    return o


def _hg_kernel(*refs):
    n_in = 7 * HG_BLOCKS
    data = refs[:n_in]
    trif_ref, tsmf_ref, mf_ref, trib_ref, tsmb_ref, mb_ref = refs[n_in:n_in + 6]
    outs = refs[n_in + 6:n_in + 6 + 2 * HG_BLOCKS]
    st_ref, g_ref = refs[-2:]

    @pl.when(pl.program_id(1) == 0)
    def _():
        st_ref[...] = jnp.zeros_like(st_ref)

    C = CHUNK
    for sub in range(HG_SUB):
        k = 0
        for u in range(HG_BLOCKS):
            qf_ref, ff_ref, vf_ref, qb_ref, fb_ref, vb_ref, lb_ref = data[7 * u:7 * u + 7]
            of_ref, ob_ref = outs[2 * u:2 * u + 2]
            dirs = ((qf_ref, ff_ref, vf_ref, trif_ref, tsmf_ref, mf_ref, of_ref, False),
                    (qb_ref, fb_ref, vb_ref, trib_ref, tsmb_ref, mb_ref, ob_ref, True))
            for q_ref, f_ref, v_ref, tri_ref, tsm_ref, m_ref, o_ref, rev in dirs:
                r0 = (HG_SUB - 1 - sub if rev else sub) * C
                rows = slice(r0, r0 + C)
                for hh in range(HG_PAIR):
                    sl = slice(hh * LANE, (hh + 1) * LANE)
                    o = _hg_chain(q_ref[rows, sl], f_ref[rows, sl], v_ref[rows, sl], lb_ref[:, sl],
                                  tri_ref, tsm_ref, m_ref, g_ref.at[k], st_ref.at[k], rev)
                    o_ref[rows, sl] = o.astype(o_ref.dtype)
                    k += 1


def _hgrn(proj, lb, consts_f, consts_b):
    s = proj.shape[0]
    C = HG_SUB * CHUNK
    nc = s // C
    w = HG_PAIR * LANE
    nb = HG_BLOCKS
    const = lambda a: pl.BlockSpec(a.shape, lambda h, c: tuple(0 for _ in a.shape))
    consts = list(consts_f) + list(consts_b)
    in_specs, args, out_specs = [], [], []
    for u in range(nb):
        fwd = lambda off, u=u: pl.BlockSpec((C, w), lambda h, c: (c, off // w + nb * h + u))
        bwd = lambda off, u=u: pl.BlockSpec((C, w), lambda h, c: (nc - 1 - c, off // w + nb * h + u))
        in_specs += [fwd(OFF_Q), fwd(OFF_FF), fwd(OFF_I), bwd(OFF_Q), bwd(OFF_FB), bwd(OFF_I),
                     pl.BlockSpec((1, w), lambda h, c, u=u: (0, nb * h + u))]
        args += [proj] * 6 + [lb]
        out_specs += [pl.BlockSpec((C, w), lambda h, c: (c, h)),
                      pl.BlockSpec((C, w), lambda h, c: (nc - 1 - c, h))]
    n_chain = 2 * nb * HG_PAIR
    outs = pl.pallas_call(
        _hg_kernel,
        grid=(HG_HEADS // (HG_PAIR * nb), nc),
        in_specs=in_specs + [const(a) for a in consts],
        out_specs=out_specs,
        out_shape=[jax.ShapeDtypeStruct((s, D_MODEL // nb), bf16)] * (2 * nb),
        scratch_shapes=[pltpu.VMEM((n_chain, HG_HEAD_DIM, HG_HEAD_DIM), f32),
                        pltpu.VMEM((n_chain, CHUNK, LANE), f32)],
        compiler_params=_params(("parallel", "arbitrary")),
        name="hgrn",
    )(*args, *consts)
    return outs


HG_GATE_COLS = 256


def _hg_out_kernel(*refs):
    nb = HG_BLOCKS
    scan_refs, w_ref = refs[:2 * nb], refs[2 * nb]
    g_refs, y_ref = refs[2 * nb + 1:-1], refs[-1]
    w = HG_PAIR * LANE
    parts = []
    for u in range(nb):
        parts.append(scan_refs[2 * u][...].astype(f32) + scan_refs[2 * u + 1][...].astype(f32))
    cols = [parts[blk % nb][:, (blk // nb) * w:(blk // nb + 1) * w] for blk in range(D_MODEL // w)]
    o = jnp.concatenate(cols, axis=1)
    g = jnp.concatenate([r[...] for r in g_refs], axis=1).astype(f32)
    rs = lax.rsqrt(jnp.mean(o * o, axis=-1, keepdims=True) + RMS_EPS)
    y_ref[...] = (o * rs * w_ref[...] * _silu(g)).astype(y_ref.dtype)


def _hg_out(scans, proj, norm_w):
    s = proj.shape[0]
    d = D_MODEL
    tm = 256
    cw = HG_GATE_COLS
    tile = pl.BlockSpec((tm, d), lambda i: (i, 0))
    part = pl.BlockSpec((tm, d // HG_BLOCKS), lambda i: (i, 0))
    gates = [pl.BlockSpec((tm, cw), functools.partial(lambda i, u: (i, OFF_G // cw + u), u=u))
             for u in range(d // cw)]
    return pl.pallas_call(
        _hg_out_kernel,
        grid=(s // tm,),
        in_specs=[part] * len(scans) + [pl.BlockSpec((1, d), lambda i: (0, 0))] + gates,
        out_specs=tile,
        out_shape=jax.ShapeDtypeStruct((s, d), bf16),
        compiler_params=_params(("parallel",)),
        name="hgrn_out",
    )(*scans, norm_w.reshape(1, d), *([proj] * len(gates)))


def _post_mix_kernel(y_ref, x_ref, gm_ref, wpost_ref, wpre_ref, sc_ref, sh_ref, wr_ref,
                     x1_ref, hp_ref, lg_ref):
    y = y_ref[...]
    yn = y * lax.rsqrt(jnp.mean(y * y, axis=-1, keepdims=True) + RMS_EPS) * wpost_ref[...]
    x1 = x_ref[...] + gm_ref[...] * yn
    x1_ref[...] = x1
    h = x1 * lax.rsqrt(jnp.mean(x1 * x1, axis=-1, keepdims=True) + RMS_EPS) * wpre_ref[...]
    h = h * (1.0 + sc_ref[...]) + sh_ref[...]
    hhi, hlo = _split2(h)
    whi, wlo = _split2(wr_ref[...])
    lg_ref[...] = _dot(hhi, whi) + _dot(hhi, wlo) + _dot(hlo, whi)
    half = h.shape[1] // 2
    top = pltpu.bitcast(hhi[:, :half].astype(f32), jnp.uint32)
    bot = pltpu.bitcast(hhi[:, half:].astype(f32), jnp.uint32)
    hp_ref[...] = top | (bot >> 16)


def _post_mix(y, x, gm, wpost, wpre, sc, sh, wr_pad):
    s, d = x.shape
    tm = 256
    row = pl.BlockSpec((1, d), lambda i: (0, 0))
    tile = pl.BlockSpec((tm, d), lambda i: (i, 0))
    return pl.pallas_call(
        _post_mix_kernel,
        grid=(s // tm,),
        in_specs=[tile, tile, row, row, row, row, row, pl.BlockSpec((d, LANE), lambda i: (0, 0))],
        out_specs=[tile, pl.BlockSpec((tm, d // 2), lambda i: (i, 0)),
                   pl.BlockSpec((tm, LANE), lambda i: (i, 0))],
        out_shape=[jax.ShapeDtypeStruct((s, d), f32), jax.ShapeDtypeStruct((s, d // 2), jnp.uint32),
                   jax.ShapeDtypeStruct((s, LANE), f32)],
        compiler_params=_params(("parallel",), VMEM_LIMIT),
        name="post_mix",
    )(y, x, gm, wpost.reshape(1, d), wpre.reshape(1, d), sc, sh, wr_pad)


def _select_kernel(lg_ref, utri_ref, pos_ref, sel_ref, aff_ref, *, seq, cap):
    E = N_EXPERTS
    nb = seq // LANE
    lg = lg_ref[...]
    lane = lax.broadcasted_iota(jnp.int32, lg.shape, 1)
    lg = jnp.where(lane < E, lg, -jnp.inf)
    mx = jnp.max(lg, axis=-1, keepdims=True)
    ex = jnp.exp(lg - mx)
    aff = ex / jnp.sum(ex, axis=-1, keepdims=True)
    aff_t = aff.T[:E, :]
    aff_ref[...] = aff_t
    bits = pltpu.bitcast(aff_t, jnp.int32)

    def search(i, thr):
        cand = thr | (jnp.int32(1) << (30 - i))
        cnt = jnp.sum((bits >= cand).astype(jnp.int32), axis=-1, keepdims=True)
        return jnp.where(cnt >= cap, cand, thr)

    thr = lax.fori_loop(0, 31, search, jnp.zeros((E, 1), jnp.int32))
    gt = bits > thr
    eq = bits == thr
    need = cap - jnp.sum(gt.astype(jnp.int32), axis=-1, keepdims=True)
    utri = utri_ref[...]

    def cumsum_excl(mask_f):
        carry = jnp.zeros((E, 1), f32)
        outs = []
        for b in range(nb):
            blk = mask_f[:, b * LANE:(b + 1) * LANE]
            inc = _dot(blk.astype(bf16), utri)
            outs.append(inc - blk + carry)
            carry = carry + inc[:, LANE - 1:LANE]
        return jnp.concatenate(outs, axis=1)

    eq_f = eq.astype(f32)
    rank_eq = cumsum_excl(eq_f)
    sel = gt | (eq & (rank_eq < need.astype(f32)))
    sel_f = sel.astype(f32)
    pos = cumsum_excl(sel_f)
    pos_ref[...] = pos
    sel_ref[...] = sel_f


def _select(logits, cap):
    s = logits.shape[0]
    t = np.arange(LANE)
    utri = jnp.asarray((t[:, None] <= t[None, :]).astype(np.float32), bf16)
    full = lambda shape: pl.BlockSpec(shape, lambda: tuple(0 for _ in shape))
    rows = jax.ShapeDtypeStruct((N_EXPERTS, s), f32)
    return pl.pallas_call(
        functools.partial(_select_kernel, seq=s, cap=cap),
        in_specs=[full((s, LANE)), full((LANE, LANE))],
        out_specs=[full((N_EXPERTS, s))] * 3,
        out_shape=[rows, rows, rows],
        compiler_params=_params(None, VMEM_LIMIT),
        name="ec_select",
    )(logits, utri)


def _extract_plan(pos, sel, cap, group):
    incl_end = (pos + sel)[:, LANE - 1::LANE]
    before = jnp.concatenate([jnp.zeros_like(incl_end[:, :1]), incl_end[:, :-1]], axis=1)
    p0 = (jnp.arange(cap // group) * group).astype(f32)
    lo = jnp.sum(incl_end[:, None, :] <= p0[None, :, None], axis=-1)
    hi = jnp.sum(before[:, None, :] < (p0 + group)[None, :, None], axis=-1)
    return lo.astype(jnp.int32), hi.astype(jnp.int32)


def _extract_kernel(lo_ref, hi_ref, pos_ref, sel_ref, aff_ref, idx_ref, gate_ref, *, cap, group):
    e = pl.program_id(0)
    prow = lax.broadcasted_iota(jnp.int32, (group, LANE), 0).astype(f32)
    for c in range(cap // group):
        pcol = prow + float(c * group)
        lo = lo_ref[e, c]

        def body(b, carry):
            cnt, g = carry
            c0 = pl.multiple_of(b * LANE, LANE)
            selr = sel_ref[0, :, pl.ds(c0, LANE)]
            incl = pos_ref[0, :, pl.ds(c0, LANE)] + selr
            affr = aff_ref[0, :, pl.ds(c0, LANE)]
            cnt = cnt + jnp.where(incl <= pcol, 1.0, 0.0)
            g = g + jnp.where((selr > 0.0) & (incl == pcol + 1.0), affr, 0.0)
            return cnt, g

        zero = jnp.zeros((group, LANE), f32)
        cnt, g = lax.fori_loop(lo, hi_ref[e, c], body, (zero, zero))
        idx_col = jnp.sum(cnt, axis=-1, keepdims=True) + (lo * LANE).astype(f32)
        idx_ref[0, c * group:(c + 1) * group, :] = idx_col.astype(jnp.int32)
        gate_ref[0, c * group:(c + 1) * group, :] = jnp.sum(g, axis=-1, keepdims=True)


def _extract(pos, sel, aff, cap):
    s = pos.shape[1]
    group = min(LANE, cap)
    lo, hi = _extract_plan(pos, sel, cap, group)
    full = pl.BlockSpec((1, 1, s), lambda e, *_: (e, 0, 0))
    col = pl.BlockSpec((1, cap, 1), lambda e, *_: (e, 0, 0))
    grid_spec = pltpu.PrefetchScalarGridSpec(
        num_scalar_prefetch=2, grid=(N_EXPERTS,),
        in_specs=[full, full, full], out_specs=[col, col])
    return pl.pallas_call(
        functools.partial(_extract_kernel, cap=cap, group=group),
        grid_spec=grid_spec,
        out_shape=[jax.ShapeDtypeStruct((N_EXPERTS, cap, 1), jnp.int32),
                   jax.ShapeDtypeStruct((N_EXPERTS, cap, 1), f32)],
        compiler_params=_params(("parallel",)),
        name="ec_extract",
    )(lo, hi, *[a.reshape(N_EXPERTS, 1, s) for a in (pos, sel, aff)])


FF_TILE = 256
DOWN_TILE = 512


def _expert_kernel(idx_ref, hp_ref, wg_ref, wu_ref, wd_ref, gate_ref, y_ref,
                   xbuf_ref, xa_ref, xb_ref, hid_ref, sem, *, cap):
    e = pl.program_id(0)
    s = pl.program_id(1)
    n_ff = D_FF_EXPERT // FF_TILE
    half = D_MODEL // 2

    def gather(ex):
        def start(p, c):
            pltpu.make_async_copy(hp_ref.at[pl.ds(idx_ref[ex, p], 1), :],
                                  xbuf_ref.at[pl.ds(p, 1), :], sem).start()
            return c

        lax.fori_loop(0, cap, start, 0, unroll=8)

    @pl.when((s == 0) & (e == 0))
    def _():
        gather(0)

    @pl.when(s == 0)
    def _():
        pltpu.make_async_copy(hp_ref.at[pl.ds(0, cap), :], xbuf_ref, sem).wait()
        u = xbuf_ref[...]
        xa_ref[...] = pltpu.bitcast(u & jnp.uint32(0xFFFF0000), f32).astype(bf16)
        xb_ref[...] = pltpu.bitcast(u << 16, f32).astype(bf16)

    @pl.when((s == 0) & (e + 1 < N_EXPERTS))
    def _():
        gather(e + 1)

    @pl.when(s < n_ff)
    def _():
        wg = wg_ref[0].astype(bf16)
        wu = wu_ref[0].astype(bf16)
        xa = xa_ref[...]
        xb = xb_ref[...]
        hg = _dot(xa, wg[:half]) + _dot(xb, wg[half:])
        hu = _dot(xa, wu[:half]) + _dot(xb, wu[half:])
        c0 = pl.multiple_of(s * FF_TILE, FF_TILE)
        hid_ref[:, pl.ds(c0, FF_TILE)] = (_silu(hg) * hu).astype(bf16)

    @pl.when(s >= n_ff)
    def _():
        y = _dot(hid_ref[...], wd_ref[0].astype(bf16))
        y_ref[0] = (y * gate_ref[0]).astype(y_ref.dtype)


def _experts(idx, hp, w_gate, w_up, w_down, gate, cap):
    n_ff = D_FF_EXPERT // FF_TILE
    n_dn = D_MODEL // DOWN_TILE
    ffj = lambda s: jnp.minimum(s, n_ff - 1)
    dnj = lambda s: jnp.maximum(s - n_ff, 0)
    grid_spec = pltpu.PrefetchScalarGridSpec(
        num_scalar_prefetch=1,
        grid=(N_EXPERTS, n_ff + n_dn),
        in_specs=[pl.BlockSpec(memory_space=pl.ANY),
                  pl.BlockSpec((1, D_MODEL, FF_TILE), lambda e, s, idx: (e, 0, ffj(s))),
                  pl.BlockSpec((1, D_MODEL, FF_TILE), lambda e, s, idx: (e, 0, ffj(s))),
                  pl.BlockSpec((1, D_FF_EXPERT, DOWN_TILE), lambda e, s, idx: (e, 0, dnj(s))),
                  pl.BlockSpec((1, cap, 1), lambda e, s, idx: (e, 0, 0))],
        out_specs=pl.BlockSpec((1, cap, DOWN_TILE), lambda e, s, idx: (e, 0, dnj(s))),
        scratch_shapes=[pltpu.VMEM((cap, D_MODEL // 2), jnp.uint32),
                        pltpu.VMEM((cap, D_MODEL // 2), bf16),
                        pltpu.VMEM((cap, D_MODEL // 2), bf16),
                        pltpu.VMEM((cap, D_FF_EXPERT), bf16),
                        pltpu.SemaphoreType.DMA(())],
    )
    return pl.pallas_call(
        functools.partial(_expert_kernel, cap=cap),
        grid_spec=grid_spec,
        out_shape=jax.ShapeDtypeStruct((N_EXPERTS, cap, D_MODEL), bf16),
        compiler_params=_params(("arbitrary", "arbitrary"), VMEM_LIMIT),
        name="ec_experts",
    )(idx, hp, w_gate, w_up, w_down, gate)


COMBINE_ALIGN = 16


COMBINE_K = 256


def _combine_plan(pos, cap, tt):
    al = COMBINE_ALIGN
    first = pos[:, ::tt].astype(jnp.int32)
    nxt = jnp.concatenate([first[:, 1:], jnp.full((first.shape[0], 1), cap, jnp.int32)], axis=1)
    start = first // al * al
    pieces = jnp.where(nxt > first, (nxt - start + al - 1) // al, 0)
    base = (jnp.cumsum(pieces, axis=0) - pieces) * al
    total = jnp.sum(pieces, axis=0)
    chunks = (total * al + COMBINE_K - 1) // COMBINE_K
    return start, pieces, base, total, chunks


def _combine_kernel(start_ref, pieces_ref, base_ref, total_ref, chunks_ref,
                    y_ref, pos_ref, sel_ref, x1_ref, gf_ref, w_ref, o_ref,
                    stage_ref, acc_ref, sem, *, tt):
    i = pl.program_id(0)
    al = COMBINE_ALIGN

    @pl.when(i == 0)
    def _():
        stage_ref[...] = jnp.zeros_like(stage_ref)

    for e in range(N_EXPERTS):
        def issue(j, c):
            src = pl.multiple_of(start_ref[e, i] + j * al, al)
            dst = pl.multiple_of(base_ref[e, i] + j * al, al)
            pltpu.make_async_copy(y_ref.at[e, pl.ds(src, al), :], stage_ref.at[pl.ds(dst, al), :], sem).start()
            return c

        lax.fori_loop(0, pieces_ref[e, i], issue, 0)

    keys = []
    for e in range(N_EXPERTS):
        shift = (base_ref[e, i] - start_ref[e, i]).astype(f32)
        keys.append(jnp.where(sel_ref[e:e + 1, :] > 0.0, pos_ref[e:e + 1, :] + shift, -1.0))

    def wait(j, c):
        pltpu.make_async_copy(y_ref.at[0, pl.ds(0, al), :], stage_ref.at[pl.ds(0, al), :], sem).wait()
        return c

    lax.fori_loop(0, total_ref[i], wait, 0)
    acc_ref[...] = jnp.zeros_like(acc_ref)
    riota = lax.broadcasted_iota(jnp.int32, (COMBINE_K, tt), 0).astype(f32)

    def chunk(k, c):
        r0 = pl.multiple_of(k * COMBINE_K, COMBINE_K)
        rows = riota + r0.astype(f32)
        et = jnp.zeros((COMBINE_K, tt), f32)
        for e in range(N_EXPERTS):
            et = jnp.where(keys[e] == rows, 1.0, et)
        acc_ref[...] += lax.dot_general(et.astype(bf16), stage_ref[pl.ds(r0, COMBINE_K), :],
                                        (((0,), (0,)), ((), ())), preferred_element_type=f32)
        return c

    lax.fori_loop(0, chunks_ref[i], chunk, 0)
    y = acc_ref[...]
    yn = y * lax.rsqrt(jnp.mean(y * y, axis=-1, keepdims=True) + RMS_EPS) * w_ref[...]
    o_ref[...] = x1_ref[...] + gf_ref[...] * yn


def _combine(plan, yexp, pos, sel, x1, gf, w_post, tt):
    s, d = x1.shape
    stage_rows = N_EXPERTS * (tt + 2 * COMBINE_ALIGN)
    stage_rows = (stage_rows + COMBINE_K - 1) // COMBINE_K * COMBINE_K
    row = pl.BlockSpec((1, d), lambda i, *_: (0, 0))
    grid_spec = pltpu.PrefetchScalarGridSpec(
        num_scalar_prefetch=5,
        grid=(s // tt,),
        in_specs=[pl.BlockSpec(memory_space=pl.ANY),
                  pl.BlockSpec((N_EXPERTS, tt), lambda i, *_: (0, i)),
                  pl.BlockSpec((N_EXPERTS, tt), lambda i, *_: (0, i)),
                  pl.BlockSpec((tt, d), lambda i, *_: (i, 0)), row, row],
        out_specs=pl.BlockSpec((tt, d), lambda i, *_: (i, 0)),
        scratch_shapes=[pltpu.VMEM((stage_rows, d), bf16), pltpu.VMEM((tt, d), f32),
                        pltpu.SemaphoreType.DMA(())],
    )
    return pl.pallas_call(
        functools.partial(_combine_kernel, tt=tt),
        grid_spec=grid_spec,
        out_shape=jax.ShapeDtypeStruct((s, d), f32),
        compiler_params=_params(("arbitrary",), VMEM_LIMIT),
        name="ec_combine",
    )(*plan, yexp, pos, sel, x1, gf, w_post.reshape(1, d))


GATE_SUB = 256


def _gate_extras(proj, off, tm, tn):
    n_sub = tn // GATE_SUB
    return [(proj, (tm, GATE_SUB), functools.partial(
        lambda m, j, u: (m, off // GATE_SUB + j * n_sub + u), u=u)) for u in range(n_sub)]


def _merge_a(acc, *gate_refs):
    ga = jnp.concatenate([r[...] for r in gate_refs], axis=1).astype(f32)
    return _sigmoid(ga) * acc


def _merge_b(acc, part_ref, *gate_refs):
    gb = jnp.concatenate([r[...] for r in gate_refs], axis=1).astype(f32)
    return part_ref[...] + _sigmoid(gb) * acc


def _layer(x, mod, lb, norm_pre_mix, norm_post_mix, norm_pre_ffn, norm_post_ffn,
           w_in, conv_w, conv_b, dt_bias, a_log, d_skip, ssd_norm_w, hg_norm_w,
           w_ssd_out, w_hg_out, w_mix_out, w_router, w_gate, w_up, w_down):
    s, d = x.shape
    sh_m, sc_m, g_m, sh_f, sc_f, g_f = [mod[:, i * d:(i + 1) * d] for i in range(6)]

    h = _prenorm(x, norm_pre_mix, sc_m, sh_m)
    tm8, tm16 = s // 8, s // 16
    proj = _wsmm(h, w_in, tm=tm8, tn=1280, out_dtype=bf16, name="in_proj")

    xact = _conv(proj, conv_w, conv_b)
    tri_f, tri_b, sel_a, sel_w, ea, ew = _ssd_consts()
    cb = lambda a: jnp.asarray(a, bf16)
    paf, pwf, arf, pab, pwb, arb = _ssd_prep(proj, dt_bias, a_log,
                                             (cb(tri_f), cb(tri_b), cb(sel_a), cb(sel_w)))
    y_f = _ssd(xact, paf, pwf, arf, cb(ea), cb(ew), rev=False)
    dskip = jnp.repeat(d_skip, SSD_HEAD_DIM).reshape(1, SSD_D_INNER)
    y_ssd = _ssd(xact, pab, pwb, arb, cb(ea), cb(ew), rev=True,
                 final_args=(y_f, proj, ssd_norm_w.reshape(1, SSD_D_INNER), dskip))

    hg_consts = [tuple((cb(tri), cb(tsm), jnp.asarray(msk))) for tri, tsm, msk in
                 (_hg_consts(False), _hg_consts(True))]
    scans = _hgrn(proj, lb, hg_consts[0], hg_consts[1])
    y_hg = _hg_out(scans, proj, hg_norm_w)

    part = _wsmm(y_ssd, w_ssd_out, tm=tm16, tn=512, out_dtype=f32,
                 extras=_gate_extras(proj, OFF_GA, tm16, 512),
                 epilogue=_merge_a, name="ssd_out_proj")
    merged = _wsmm(y_hg, w_hg_out, tm=tm8, tn=512, out_dtype=bf16,
                   extras=[(part, (tm8, 512), lambda m, j: (m, j))]
                   + _gate_extras(proj, OFF_GB, tm8, 512),
                   epilogue=_merge_b, name="hg_out_proj")
    y_mix = _wsmm(merged, w_mix_out, tm=tm8, tn=512, out_dtype=f32, name="mix_out_proj")

    wr_pad = jnp.pad(w_router, ((0, 0), (0, LANE - N_EXPERTS)))
    x1, hp, logits = _post_mix(y_mix, x, g_m, norm_post_mix, norm_pre_ffn, sc_f, sh_f, wr_pad)
    cap = 2 * s // N_EXPERTS
    pos, sel, aff = _select(logits, cap)
    idx, gate = _extract(pos, sel, aff, cap)
    yexp = _experts(idx.reshape(N_EXPERTS, cap), hp, w_gate, w_up, w_down, gate, cap)
    tt = min(128, cap // 2)
    plan = _combine_plan(pos, cap, tt)
    return _combine(plan, yexp, pos, sel, x1, g_f, norm_post_ffn, tt)


def kernel(x, c, w_ada, b_ada, norm_pre_mix, norm_post_mix, norm_pre_ffn, norm_post_ffn, w_in, conv_w, conv_b, dt_bias_fwd, dt_bias_bwd, a_log_fwd, a_log_bwd, d_skip, ssd_norm_w, hg_lower_bound, hg_norm_w, w_ssd_out, w_hg_out, w_mix_out, w_router, w_gate, w_up, w_down):
    depth = w_ada.shape[0]
    lower_bounds = jnp.cumsum(jax.nn.softmax(hg_lower_bound.astype(f32), axis=0), axis=0)
    outs = []
    for bi in range(x.shape[0]):
        xb = x[bi]
        cb_ = c[bi:bi + 1]
        for l in range(depth):
            mod = _ada(cb_, w_ada[l], b_ada[l])
            dt_bias = jnp.concatenate([dt_bias_fwd[l], dt_bias_bwd[l]]).reshape(1, -1)
            a_log = jnp.concatenate([a_log_fwd[l], a_log_bwd[l]]).reshape(1, -1)
            xb = _layer(xb, mod, lower_bounds[l].reshape(1, -1), norm_pre_mix[l], norm_post_mix[l],
                        norm_pre_ffn[l], norm_post_ffn[l], w_in[l], conv_w[l], conv_b[l],
                        dt_bias, a_log, d_skip[l], ssd_norm_w[l], hg_norm_w[l],
                        w_ssd_out[l], w_hg_out[l], w_mix_out[l], w_router[l],
                        w_gate[l], w_up[l], w_down[l])
        outs.append(xb)
    return jnp.stack(outs, axis=0)
```
